```python
import jax, jax.numpy as jnp
from jax import lax
import numpy as np

D_MODEL = 2048
BATCH = 4
SEQ = 4096
DEPTH = 2
DEC_BATCH = 8
DEC_SEQ = 2048
PAST_LEN = 128

GRID_W = 64
D_MIX = D_MODEL
ATTN_HD = 64
ATTN_HEADS = (D_MIX // 2) // ATTN_HD
D_ATTN = ATTN_HEADS * ATTN_HD
WIN_R = 8
WIN_C = 16
D_CONV = D_MIX // 4
CONV_K = 31
D_SGU = D_MIX - D_ATTN - D_CONV
SGU_GROUPS = 8
SGU_HD = D_SGU // SGU_GROUPS
SGU_CHUNK = 128
D_IN = 3 * D_ATTN + 2 * D_CONV + 2 * D_SGU
IN_SPLITS = [D_ATTN, 2 * D_ATTN, 3 * D_ATTN, 3 * D_ATTN + D_CONV, 3 * D_ATTN + 2 * D_CONV, 3 * D_ATTN + 2 * D_CONV + D_SGU]
N_EXPERTS = 32
TOP_K = 4
D_FF = D_MODEL
SWIGLU_LIMIT = 7.0
SWIGLU_ALPHA = 1.702
MOE_BLOCK = 256
N_MOD = 6
EPS = 1e-6

kernel_name = 'hybrid_natten_conformer_sgu_moe_encoder'


def rms_norm(x, w):
    xf = x.astype(jnp.float32)
    y = xf * lax.rsqrt(jnp.mean(xf * xf, axis=-1, keepdims=True) + EPS)
    return (y * w.astype(jnp.float32)).astype(x.dtype)


def layer_norm(x, w, b):
    xf = x.astype(jnp.float32)
    mu = jnp.mean(xf, axis=-1, keepdims=True)
    var = jnp.mean(jnp.square(xf - mu), axis=-1, keepdims=True)
    y = (xf - mu) * lax.rsqrt(var + EPS)
    return (y * w.astype(jnp.float32) + b.astype(jnp.float32)).astype(x.dtype)


def neighbourhood_attention(q, k, v, q_norm_w, k_norm_w, rel_pos_bias):
    b, n, _ = q.shape
    rows = n // GRID_W
    wr = min(WIN_R, rows)
    grid = (b, rows, GRID_W, ATTN_HEADS, ATTN_HD)
    qg = rms_norm(q.reshape(grid), q_norm_w).transpose(0, 3, 1, 2, 4)
    kg = rms_norm(k.reshape(grid), k_norm_w).transpose(0, 3, 1, 2, 4)
    vg = v.reshape(grid).transpose(0, 3, 1, 2, 4)
    col_start = np.clip(np.arange(GRID_W) - WIN_C // 2, 0, GRID_W - WIN_C)
    col_idx = col_start[:, None] + np.arange(WIN_C)[None, :]
    col_rel = col_idx - np.arange(GRID_W)[:, None] + (WIN_C - 1)
    scale = ATTN_HD ** -0.5

    def row_block(r):
        rs = jnp.clip(r - wr // 2, 0, rows - wr)
        q_r = lax.dynamic_index_in_dim(qg, r, axis=2, keepdims=False)
        k_band = lax.dynamic_slice_in_dim(kg, rs, wr, axis=2)
        v_band = lax.dynamic_slice_in_dim(vg, rs, wr, axis=2)
        k_win = k_band[:, :, :, col_idx]
        v_win = v_band[:, :, :, col_idx]
        row_rel = rs + jnp.arange(wr) - r + (WIN_R - 1)
        bias = rel_pos_bias[:, row_rel[None, :, None], col_rel[:, None, :]]
        s = jnp.einsum('bhqd,bhrqcd->bhqrc', q_r, k_win).astype(jnp.float32) * scale + bias.astype(jnp.float32)
        p = jax.nn.softmax(s.reshape(b, ATTN_HEADS, GRID_W, wr * WIN_C), axis=-1)
        p = p.reshape(s.shape).astype(v_win.dtype)
        return jnp.einsum('bhqrc,bhrqcd->bhqd', p, v_win)

    out = lax.map(row_block, jnp.arange(rows, dtype=jnp.int32))
    return out.transpose(1, 0, 3, 2, 4).reshape(b, n, D_ATTN)


def conformer_conv(a, g, conv_w, conv_b, norm_w, norm_b):
    glu = a * jax.nn.sigmoid(g)
    y = lax.conv_general_dilated(glu, conv_w[:, None, :], window_strides=(1,), padding=[(CONV_K // 2, CONV_K // 2)], dimension_numbers=('NWC', 'WIO', 'NWC'), feature_group_count=D_CONV)
    y = layer_norm(y + conv_b, norm_w, norm_b)
    return jax.nn.silu(y)


def spatial_gating(u, v, norm_w, norm_b, w_s, b_s):
    b, n, _ = u.shape
    u = jax.nn.gelu(u)
    v = layer_norm(jax.nn.gelu(v), norm_w, norm_b)
    vc = v.reshape(b, n // SGU_CHUNK, SGU_CHUNK, SGU_GROUPS, SGU_HD)
    mixed = jnp.einsum('gqp,bcpgd->bcqgd', w_s, vc) + b_s.T[None, None, :, :, None]
    return u * mixed.reshape(b, n, D_SGU)


def moe_ffn(h, router_w, router_b, w_gu, b_gu, w_dn, b_dn):
    t, d = h.shape
    logits = (h @ router_w + router_b).astype(jnp.float32)
    top_val, top_idx = lax.top_k(logits, TOP_K)
    gates = jax.nn.softmax(top_val, axis=-1).astype(h.dtype)
    n_slot = t * TOP_K
    flat_e = top_idx.reshape(-1).astype(jnp.int32)
    flat_tok = jnp.arange(n_slot, dtype=jnp.int32) // TOP_K
    flat_g = gates.reshape(-1)
    order = jnp.argsort(flat_e)
    e_sorted = flat_e[order]
    counts = jnp.bincount(flat_e, length=N_EXPERTS).astype(jnp.int32)
    start = jnp.cumsum(counts) - counts
    pcounts = (counts + MOE_BLOCK - 1) // MOE_BLOCK * MOE_BLOCK
    pend = jnp.cumsum(pcounts)
    pstart = pend - pcounts
    dest = pstart[e_sorted] + (jnp.arange(n_slot, dtype=jnp.int32) - start[e_sorted])
    n_blocks = -(-(n_slot + N_EXPERTS * (MOE_BLOCK - 1)) // MOE_BLOCK)
    n_pad = n_blocks * MOE_BLOCK
    buf_tok = jnp.full((n_pad,), t, jnp.int32).at[dest].set(flat_tok[order])
    buf_gate = jnp.zeros((n_pad,), h.dtype).at[dest].set(flat_g[order])
    blk_e = jnp.minimum(jnp.searchsorted(pend, jnp.arange(n_blocks, dtype=jnp.int32) * MOE_BLOCK, side='right'), N_EXPERTS - 1)
    h_pad = jnp.concatenate([h, jnp.zeros((1, d), h.dtype)], axis=0)

    def expert_block(args):
        tok, e = args
        xb = h_pad[tok]
        gu = xb @ w_gu[e] + b_gu[e]
        gate, up = gu[:, :D_FF], gu[:, D_FF:]
        gate = jnp.minimum(gate, SWIGLU_LIMIT)
        up = jnp.clip(up, -SWIGLU_LIMIT, SWIGLU_LIMIT)
        act = (up + 1) * (gate * jax.nn.sigmoid(SWIGLU_ALPHA * gate))
        return act @ w_dn[e] + b_dn[e]

    out = lax.map(expert_block, (buf_tok.reshape(n_blocks, MOE_BLOCK), blk_e))
    out = out.reshape(n_pad, d) * buf_gate[:, None]
    return jax.ops.segment_sum(out, buf_tok, num_segments=t + 1)[:t]


def encoder_trunk(x, c, ada_w, ada_b, norm_mix_w, norm_ffn_w, w_in, q_norm_w, k_norm_w, rel_pos_bias, conv_w, conv_b, conv_norm_w, conv_norm_b, sgu_norm_w, sgu_norm_b, sgu_w, sgu_b, mix_norm_w, w_out, router_w, router_b, w_gate_up, b_gate_up, w_down, b_down):
    b, n, d = x.shape
    for l in range(DEPTH):
        mod = jax.nn.silu(c) @ ada_w[l] + ada_b[l]
        shift1, scale1, gate1, shift2, scale2, gate2 = jnp.split(mod[:, None, :], N_MOD, axis=-1)
        h = rms_norm(x, norm_mix_w[l]) * (1 + scale1) + shift1
        q, k, v, ca, cg, su, sv = jnp.split(h @ w_in[l], IN_SPLITS, axis=-1)
        y_attn = neighbourhood_attention(q, k, v, q_norm_w[l], k_norm_w[l], rel_pos_bias[l])
        y_conv = conformer_conv(ca, cg, conv_w[l], conv_b[l], conv_norm_w[l], conv_norm_b[l])
        y_sgu = spatial_gating(su, sv, sgu_norm_w[l], sgu_norm_b[l], sgu_w[l], sgu_b[l])
        mix = jnp.concatenate([rms_norm(y_attn, mix_norm_w[l, :D_ATTN]), rms_norm(y_conv, mix_norm_w[l, D_ATTN:D_ATTN + D_CONV]), rms_norm(y_sgu, mix_norm_w[l, D_ATTN + D_CONV:])], axis=-1)
        x = x + gate1 * (mix @ w_out[l])
        h = rms_norm(x, norm_ffn_w[l]) * (1 + scale2) + shift2
        y_ffn = moe_ffn(h.reshape(b * n, d), router_w[l], router_b[l], w_gate_up[l], b_gate_up[l], w_down[l], b_down[l]).reshape(b, n, d)
        x = x + gate2 * y_ffn
    return x


def setup_inputs(seed: int = 0) -> dict:
    key = jax.random.key(seed)
    ks = jax.random.split(key, 28)
    L = DEPTH

    def nrm(k, shape, std):
        return jax.random.normal(k, shape, jnp.float32) * std

    return {
        'x_prompt': nrm(ks[0], (BATCH, SEQ, D_MODEL), 1.0),
        'x_sample': nrm(ks[1], (DEC_BATCH, DEC_SEQ, D_MODEL), 1.0),
        'c_prompt': nrm(ks[2], (BATCH, D_MODEL), 1.0),
        'c_sample': nrm(ks[3], (DEC_BATCH, D_MODEL), 1.0),
        'ada_w': nrm(ks[4], (L, D_MODEL, N_MOD * D_MODEL), 0.5 * D_MODEL ** -0.5),
        'ada_b': nrm(ks[5], (L, N_MOD * D_MODEL), 0.02),
        'norm_mix_w': 1.0 + nrm(ks[6], (L, D_MODEL), 0.02),
        'norm_ffn_w': 1.0 + nrm(ks[7], (L, D_MODEL), 0.02),
        'w_in': nrm(ks[8], (L, D_MODEL, D_IN), D_MODEL ** -0.5),
        'q_norm_w': 1.0 + nrm(ks[9], (L, ATTN_HD), 0.02),
        'k_norm_w': 1.0 + nrm(ks[10], (L, ATTN_HD), 0.02),
        'rel_pos_bias': nrm(ks[11], (L, ATTN_HEADS, 2 * WIN_R - 1, 2 * WIN_C - 1), 0.1),
        'conv_w': nrm(ks[12], (L, CONV_K, D_CONV), CONV_K ** -0.5),
        'conv_b': nrm(ks[13], (L, D_CONV), 0.02),
        'conv_norm_w': 1.0 + nrm(ks[14], (L, D_CONV), 0.02),
        'conv_norm_b': nrm(ks[15], (L, D_CONV), 0.02),
        'sgu_norm_w': 1.0 + nrm(ks[16], (L, D_SGU), 0.02),
        'sgu_norm_b': nrm(ks[17], (L, D_SGU), 0.02),
        'sgu_w': nrm(ks[18], (L, SGU_GROUPS, SGU_CHUNK, SGU_CHUNK), SGU_CHUNK ** -0.5),
        'sgu_b': 1.0 + nrm(ks[19], (L, SGU_GROUPS, SGU_CHUNK), 0.02),
        'mix_norm_w': 1.0 + nrm(ks[20], (L, D_MIX), 0.02),
        'w_out': nrm(ks[21], (L, D_MIX, D_MODEL), D_MIX ** -0.5),
        'router_w': nrm(ks[22], (L, D_MODEL, N_EXPERTS), D_MODEL ** -0.5),
        'router_b': nrm(ks[23], (L, N_EXPERTS), 0.01),
        'w_gate_up': nrm(ks[24], (L, N_EXPERTS, D_MODEL, 2 * D_FF), D_MODEL ** -0.5),
        'b_gate_up': nrm(ks[25], (L, N_EXPERTS, 2 * D_FF), 0.02),
        'w_down': nrm(ks[26], (L, N_EXPERTS, D_FF, D_MODEL), D_FF ** -0.5),
        'b_down': nrm(ks[27], (L, N_EXPERTS, D_MODEL), 0.02),
    }


def reference(x_prompt, x_sample, c_prompt, c_sample, ada_w, ada_b, norm_mix_w, norm_ffn_w, w_in, q_norm_w, k_norm_w, rel_pos_bias, conv_w, conv_b, conv_norm_w, conv_norm_b, sgu_norm_w, sgu_norm_b, sgu_w, sgu_b, mix_norm_w, w_out, router_w, router_b, w_gate_up, b_gate_up, w_down, b_down):
    y_prompt = encoder_trunk(x_prompt, c_prompt, ada_w, ada_b, norm_mix_w, norm_ffn_w, w_in, q_norm_w, k_norm_w, rel_pos_bias, conv_w, conv_b, conv_norm_w, conv_norm_b, sgu_norm_w, sgu_norm_b, sgu_w, sgu_b, mix_norm_w, w_out, router_w, router_b, w_gate_up, b_gate_up, w_down, b_down)
    y_sample = encoder_trunk(x_sample, c_sample, ada_w, ada_b, norm_mix_w, norm_ffn_w, w_in, q_norm_w, k_norm_w, rel_pos_bias, conv_w, conv_b, conv_norm_w, conv_norm_b, sgu_norm_w, sgu_norm_b, sgu_w, sgu_b, mix_norm_w, w_out, router_w, router_b, w_gate_up, b_gate_up, w_down, b_down)
    return (y_prompt, y_sample)
```

```python
import functools

import numpy as np
import jax
import jax.numpy as jnp
from jax import lax
from jax.experimental import pallas as pl
from jax.experimental.pallas import tpu as pltpu

F32 = jnp.float32
BF16 = jnp.bfloat16
U32 = jnp.uint32
I32 = jnp.int32

D_MODEL = 2048
GRID_W = 64
ATTN_HEADS = 16
ATTN_HD = 64
D_ATTN = ATTN_HEADS * ATTN_HD
WIN_R = 8
WIN_C = 16
D_CONV = 512
CONV_K = 31
D_SGU = 512
SGU_GROUPS = 8
SGU_CHUNK = 128
D_IN = 3 * D_ATTN + 2 * D_CONV + 2 * D_SGU
N_EXPERTS = 32
TOP_K = 4
D_FF = 2048
SWIGLU_LIMIT = 7.0
SWIGLU_ALPHA = 1.702
N_MOD = 6
EPS = 1e-6

SEG = 2048
SEG_ROWS = SEG // GRID_W
HALF = D_MODEL // 2
NEG_BIAS = -1e30
HI_MASK = 0xFFFF0000

V7X_VMEM_LIMIT_BYTES = 56 * 1024 * 1024

TN_ADA = 1024
TM_IN, TN_IN = 1024, 512
TM_SGU = 1024
TM_OUT, TK_OUT = 512, 512
TT_RANK = 512
TT_DISP = 512
TM_EXP, TF_EXP = 1024, 256
TT_COMB = 256
CONV_ROWS = 32
HALO = 16


def _cparams(sem):
    return pltpu.CompilerParams(dimension_semantics=sem, vmem_limit_bytes=V7X_VMEM_LIMIT_BYTES)


def _batch_of_tile(i, tile, n_prompt_seg):
    seg = (i * tile) // SEG
    return jnp.where(seg < n_prompt_seg, seg // 2, seg - n_prompt_seg // 2)


def _pack_bf16_pairs(y):
    yb = y.astype(BF16)
    lo = pltpu.bitcast(yb[:, :HALF].astype(F32), U32) >> 16
    hi = pltpu.bitcast(yb[:, HALF:].astype(F32), U32) & jnp.uint32(HI_MASK)
    return lo | hi


def _unpack_lo(u):
    return pltpu.bitcast(u << 16, F32)


def _unpack_hi(u):
    return pltpu.bitcast(u & jnp.uint32(HI_MASK), F32)


def _adaln_kernel(c_ref, w_ref, b_ref, o_ref):
    sc = jax.nn.silu(c_ref[...]).astype(BF16)
    o_ref[0] = jnp.dot(sc, w_ref[0].astype(BF16), preferred_element_type=F32) + b_ref[0]


def _adaln(c_pad, ada_w, ada_b):
    depth = ada_w.shape[0]
    nb = c_pad.shape[0]
    n_out = N_MOD * D_MODEL
    return pl.pallas_call(
        _adaln_kernel,
        grid=(depth, n_out // TN_ADA),
        in_specs=[
            pl.BlockSpec((nb, D_MODEL), lambda l, j: (0, 0)),
            pl.BlockSpec((1, D_MODEL, TN_ADA), lambda l, j: (l, 0, j)),
            pl.BlockSpec((1, 1, TN_ADA), lambda l, j: (l, 0, j)),
        ],
        out_specs=pl.BlockSpec((1, nb, TN_ADA), lambda l, j: (l, 0, j)),
        out_shape=jax.ShapeDtypeStruct((depth, nb, n_out), F32),
        compiler_params=_cparams(("arbitrary", "arbitrary")),
        name="adaln",
    )(c_pad, ada_w, ada_b.reshape(depth, 1, n_out))


def _inproj_kernel(x_ref, nw_ref, shift_ref, scale_ref, w_ref, qn_ref, kn_ref, ones_ref, o_ref, h_ref):
    j = pl.program_id(1)

    @pl.when(j == 0)
    def _():
        xf = x_ref[...]
        y = xf * lax.rsqrt(jnp.mean(xf * xf, axis=-1, keepdims=True) + EPS)
        h = (y * nw_ref[0]) * (1.0 + scale_ref[0]) + shift_ref[0]
        h_ref[...] = h.astype(BF16)

    acc = jnp.dot(h_ref[...], w_ref[0], preferred_element_type=F32)
    n_qk = 2 * D_ATTN // TN_IN

    @pl.when(j < n_qk)
    def _():
        ss = jnp.dot((acc * acc).astype(BF16), ones_ref[...], preferred_element_type=F32)
        r = lax.rsqrt(ss * (1.0 / ATTN_HD) + EPS)
        nw = jnp.where(j < n_qk // 2, qn_ref[...], kn_ref[...])
        o_ref[...] = (acc * r * nw).astype(o_ref.dtype)

    @pl.when(j >= n_qk)
    def _():
        o_ref[...] = acc.astype(o_ref.dtype)


def _inproj(x, mod_rows, norm_w, w_bf16, qn, kn, ones_bd, layer, n_prompt_seg, nb_pad):
    t = x.shape[0]
    depth = norm_w.shape[0]

    def mod_map(k):
        def f(i, j):
            b = _batch_of_tile(i, TM_IN, n_prompt_seg)
            return ((layer * nb_pad + b) * N_MOD + k, 0, 0)
        return f

    return pl.pallas_call(
        _inproj_kernel,
        grid=(t // TM_IN, D_IN // TN_IN),
        in_specs=[
            pl.BlockSpec((TM_IN, D_MODEL), lambda i, j: (i, 0)),
            pl.BlockSpec((1, 1, D_MODEL), lambda i, j: (layer, 0, 0)),
            pl.BlockSpec((1, 1, D_MODEL), mod_map(0)),
            pl.BlockSpec((1, 1, D_MODEL), mod_map(1)),
            pl.BlockSpec((1, D_MODEL, TN_IN), lambda i, j: (layer, 0, j)),
            pl.BlockSpec((1, TN_IN), lambda i, j: (0, 0)),
            pl.BlockSpec((1, TN_IN), lambda i, j: (0, 0)),
            pl.BlockSpec((TN_IN, TN_IN), lambda i, j: (0, 0)),
        ],
        out_specs=pl.BlockSpec((TM_IN, TN_IN), lambda i, j: (i, j)),
        out_shape=jax.ShapeDtypeStruct((t, D_IN), BF16),
        scratch_shapes=[pltpu.VMEM((TM_IN, D_MODEL), BF16)],
        compiler_params=_cparams(("arbitrary", "arbitrary")),
        name="inproj",
    )(x, norm_w.reshape(depth, 1, D_MODEL), mod_rows, mod_rows, w_bf16, qn, kn, ones_bd)


def _attn_kernel(q_ref, k_ref, v_ref, b_ref, o_ref, *, n_prompt_seg):
    seg = pl.program_id(0)
    is_prompt = seg < n_prompt_seg
    half = seg % 2
    rows = jnp.where(is_prompt, 2 * SEG_ROWS, SEG_ROWS)
    row0 = jnp.where(is_prompt, half * SEG_ROWS, 0)
    kv0 = jnp.where(is_prompt, 0, half * SEG)
    lane = lax.broadcasted_iota(I32, (GRID_W, 2 * ATTN_HD), 1)
    first_head = lane < ATTN_HD
    band = WIN_R * GRID_W

    def row_body(rr, carry):
        r = row0 + rr
        rs = jnp.clip(r - WIN_R // 2, 0, rows - WIN_R)
        di = r - rs
        q0 = pl.multiple_of(rr * GRID_W, GRID_W)
        k0 = pl.multiple_of(kv0 + rs * GRID_W, GRID_W)
        q = q_ref[pl.ds(q0, GRID_W), :]
        kb = k_ref[pl.ds(k0, band), :]
        vb = v_ref[pl.ds(k0, band), :]
        outs = []
        for hh in range(2):
            qm = jnp.where(first_head if hh == 0 else jnp.logical_not(first_head), q, jnp.zeros_like(q))
            s = lax.dot_general(qm, kb, (((1,), (1,)), ((), ())), preferred_element_type=F32)
            s = s + b_ref[0, hh, di]
            m = jnp.max(s, axis=-1, keepdims=True)
            p = jnp.exp(s - m)
            denom = jnp.sum(p, axis=-1, keepdims=True)
            o = jnp.dot(p.astype(BF16), vb, preferred_element_type=F32)
            outs.append(o * (1.0 / denom))
        o_ref[pl.ds(q0, GRID_W), :] = jnp.where(first_head, outs[0], outs[1]).astype(o_ref.dtype)
        return carry

    lax.fori_loop(0, SEG_ROWS, row_body, 0)


def _attention(p, bias_tab, layer, n_prompt_seg):
    t = p.shape[0]
    n_seg = t // SEG
    n_pair = ATTN_HEADS // 2
    lanes = 2 * ATTN_HD
    return pl.pallas_call(
        functools.partial(_attn_kernel, n_prompt_seg=n_prompt_seg),
        grid=(n_seg, n_pair),
        in_specs=[
            pl.BlockSpec((SEG, lanes), lambda s, h: (s, h)),
            pl.BlockSpec((2 * SEG, lanes), lambda s, h: (s // 2, n_pair + h)),
            pl.BlockSpec((2 * SEG, lanes), lambda s, h: (s // 2, 2 * n_pair + h)),
            pl.BlockSpec((1, 2, WIN_R, GRID_W, WIN_R * GRID_W), lambda s, h: (layer, h, 0, 0, 0)),
        ],
        out_specs=pl.BlockSpec((SEG, lanes), lambda s, h: (s, h)),
        out_shape=jax.ShapeDtypeStruct((t, D_ATTN), BF16),
        compiler_params=_cparams(("arbitrary", "arbitrary")),
        name="attention",
    )(p, p, p, bias_tab)


def _bias_table(rel_pos_bias):
    c = np.arange(GRID_W)
    cs = np.clip(c - WIN_C // 2, 0, GRID_W - WIN_C)
    kc = np.arange(GRID_W)
    valid = (kc[None, :] >= cs[:, None]) & (kc[None, :] < cs[:, None] + WIN_C)
    col_rel = np.clip(kc[None, :] - c[:, None] + (WIN_C - 1), 0, 2 * WIN_C - 2)
    di = np.arange(WIN_R)
    jj = np.arange(WIN_R)
    row_rel = jj[None, :] - di[:, None] + (WIN_R - 1)
    tab = rel_pos_bias[:, :, row_rel[:, None, :, None], col_rel[None, :, None, :]]
    tab = jnp.where(valid[None, None, None, :, None, :], tab.astype(F32), NEG_BIAS)
    depth = rel_pos_bias.shape[0]
    return tab.reshape(depth, ATTN_HEADS, WIN_R, GRID_W, WIN_R * GRID_W)


def _conv_kernel(am_ref, gm_ref, ap_ref, gp_ref, an_ref, gn_ref, w_ref, cb_ref, lw_ref, lb_ref, o_ref, buf_ref,
                 *, n_prompt_seg):
    seg = pl.program_id(0)
    is_prompt = seg < n_prompt_seg
    half = seg % 2
    has_prev = jnp.logical_and(is_prompt, half == 1)
    has_next = jnp.logical_and(is_prompt, half == 0)

    def glu(a_ref, g_ref):
        return a_ref[...].astype(F32) * jax.nn.sigmoid(g_ref[...].astype(F32))

    buf_ref[pl.ds(HALO, SEG), :] = glu(am_ref, gm_ref)
    buf_ref[pl.ds(0, HALO), :] = jnp.where(has_prev, glu(ap_ref, gp_ref), 0.0)
    buf_ref[pl.ds(HALO + SEG, HALO), :] = jnp.where(has_next, glu(an_ref, gn_ref), 0.0)

    w = w_ref[0]
    off = HALO - CONV_K // 2

    def chunk(ci, carry):
        r0 = pl.multiple_of(ci * CONV_ROWS, CONV_ROWS)
        win_rows = CONV_ROWS + 32
        win = buf_ref[pl.ds(r0, win_rows), :]
        acc = jnp.zeros((CONV_ROWS, D_CONV), F32)
        for s in range(8):
            ws = win if s == 0 else pltpu.roll(win, win_rows - s, axis=0)
            for a in range(4):
                k = 8 * a + s - off
                if 0 <= k < CONV_K:
                    acc = acc + ws[8 * a:8 * a + CONV_ROWS, :] * w[k:k + 1, :]
        y = acc + cb_ref[0]
        mu = jnp.mean(y, axis=-1, keepdims=True)
        yc = y - mu
        var = jnp.mean(yc * yc, axis=-1, keepdims=True)
        yn = yc * lax.rsqrt(var + EPS) * lw_ref[0] + lb_ref[0]
        o_ref[pl.ds(r0, CONV_ROWS), :] = jax.nn.silu(yn).astype(o_ref.dtype)
        return carry

    lax.fori_loop(0, SEG // CONV_ROWS, chunk, 0)


def _conv(p, conv_w, conv_b, norm_w, norm_b, layer, n_prompt_seg):
    t = p.shape[0]
    depth = conv_w.shape[0]
    n_seg = t // SEG
    ca = 3 * D_ATTN // D_CONV
    cg = ca + 1
    per_seg = SEG // HALO
    last = t // HALO - 1

    def prev_map(col):
        return lambda s: (jnp.maximum(s * per_seg - 1, 0), col)

    def next_map(col):
        return lambda s: (jnp.minimum((s + 1) * per_seg, last), col)

    vec = lambda: pl.BlockSpec((1, 1, D_CONV), lambda s: (layer, 0, 0))
    return pl.pallas_call(
        functools.partial(_conv_kernel, n_prompt_seg=n_prompt_seg),
        grid=(n_seg,),
        in_specs=[
            pl.BlockSpec((SEG, D_CONV), lambda s: (s, ca)),
            pl.BlockSpec((SEG, D_CONV), lambda s: (s, cg)),
            pl.BlockSpec((HALO, D_CONV), prev_map(ca)),
            pl.BlockSpec((HALO, D_CONV), prev_map(cg)),
            pl.BlockSpec((HALO, D_CONV), next_map(ca)),
            pl.BlockSpec((HALO, D_CONV), next_map(cg)),
            pl.BlockSpec((1, CONV_K, D_CONV), lambda s: (layer, 0, 0)),
            vec(), vec(), vec(),
        ],
        out_specs=pl.BlockSpec((SEG, D_CONV), lambda s: (s, 0)),
        out_shape=jax.ShapeDtypeStruct((t, D_CONV), BF16),
        scratch_shapes=[pltpu.VMEM((SEG + 2 * HALO, D_CONV), F32)],
        compiler_params=_cparams(("arbitrary",)),
        name="conv",
    )(p, p, p, p, p, p, conv_w, conv_b.reshape(depth, 1, D_CONV), norm_w.reshape(depth, 1, D_CONV),
      norm_b.reshape(depth, 1, D_CONV))


def _sgu_kernel(u_ref, v_ref, lw_ref, lb_ref, ws_ref, bs_ref, o_ref):
    u = jax.nn.gelu(u_ref[...].astype(F32))
    g = jax.nn.gelu(v_ref[...].astype(F32))
    mu = jnp.mean(g, axis=-1, keepdims=True)
    gc = g - mu
    var = jnp.mean(gc * gc, axis=-1, keepdims=True)
    v = (gc * lax.rsqrt(var + EPS) * lw_ref[0] + lb_ref[0]).astype(BF16)
    lanes = 2 * (D_SGU // SGU_GROUPS)
    first_group = lax.broadcasted_iota(I32, (SGU_CHUNK, lanes), 1) < lanes // 2
    for c in range(TM_SGU // SGU_CHUNK):
        rows = slice(c * SGU_CHUNK, (c + 1) * SGU_CHUNK)
        for pair in range(SGU_GROUPS // 2):
            cols = slice(pair * lanes, (pair + 1) * lanes)
            vp = v[rows, cols]
            m0 = jnp.dot(ws_ref[0, 2 * pair].astype(BF16), vp, preferred_element_type=F32)
            m1 = jnp.dot(ws_ref[0, 2 * pair + 1].astype(BF16), vp, preferred_element_type=F32)
            mixed = jnp.where(first_group, m0, m1) + bs_ref[:, cols]
            o_ref[rows, cols] = (u[rows, cols] * mixed).astype(o_ref.dtype)


def _sgu(p, norm_w, norm_b, sgu_w, bias_exp, layer):
    t = p.shape[0]
    depth = norm_w.shape[0]
    su = (3 * D_ATTN + 2 * D_CONV) // D_SGU
    sv = su + 1
    vec = lambda: pl.BlockSpec((1, 1, D_SGU), lambda i: (layer, 0, 0))
    return pl.pallas_call(
        _sgu_kernel,
        grid=(t // TM_SGU,),
        in_specs=[
            pl.BlockSpec((TM_SGU, D_SGU), lambda i: (i, su)),
            pl.BlockSpec((TM_SGU, D_SGU), lambda i: (i, sv)),
            vec(), vec(),
            pl.BlockSpec((1, SGU_GROUPS, SGU_CHUNK, SGU_CHUNK), lambda i: (layer, 0, 0, 0)),
            pl.BlockSpec((SGU_CHUNK, D_SGU), lambda i: (0, 0)),
        ],
        out_specs=pl.BlockSpec((TM_SGU, D_SGU), lambda i: (i, 0)),
        out_shape=jax.ShapeDtypeStruct((t, D_SGU), BF16),
        compiler_params=_cparams(("arbitrary",)),
        name="sgu",
    )(p, p, norm_w.reshape(depth, 1, D_SGU), norm_b.reshape(depth, 1, D_SGU), sgu_w, bias_exp)


def _rms(y, w):
    return y * lax.rsqrt(jnp.mean(y * y, axis=-1, keepdims=True) + EPS) * w


def _outproj_kernel(ya_ref, yc_ref, ys_ref, x_ref, mw_ref, g1_ref, nw_ref, sh2_ref, sc2_ref, w_ref, rwt_ref, rb_ref,
                    x1_ref, hp_ref, ti_ref, gcol_ref, mix_ref, acc_ref):
    k = pl.program_id(1)
    nk = D_MODEL // TK_OUT

    @pl.when(k == 0)
    def _():
        mw = mw_ref[0]
        na = _rms(ya_ref[...].astype(F32), mw[:, :D_ATTN])
        nc = _rms(yc_ref[...].astype(F32), mw[:, D_ATTN:D_ATTN + D_CONV])
        ns = _rms(ys_ref[...].astype(F32), mw[:, D_ATTN + D_CONV:])
        mix_ref[0] = na[:, :TK_OUT].astype(BF16)
        mix_ref[1] = na[:, TK_OUT:].astype(BF16)
        mix_ref[2] = nc.astype(BF16)
        mix_ref[3] = ns.astype(BF16)
        acc_ref[...] = jnp.zeros_like(acc_ref)

    acc_ref[...] += jnp.dot(mix_ref[k], w_ref[0], preferred_element_type=F32)

    @pl.when(k == nk - 1)
    def _():
        x1 = x_ref[...] + g1_ref[0] * acc_ref[...]
        x1_ref[...] = x1
        h2 = _rms(x1, nw_ref[0]) * (1.0 + sc2_ref[0]) + sh2_ref[0]
        hp_ref[...] = _pack_bf16_pairs(h2)

        hi = h2.astype(BF16)
        lo = (h2 - hi.astype(F32)).astype(BF16)
        rw = rwt_ref[0]
        rwh = rw.astype(BF16)
        rwl = (rw - rwh.astype(F32)).astype(BF16)
        dn = (((1,), (1,)), ((), ()))
        logits = (lax.dot_general(rwh, hi, dn, preferred_element_type=F32)
                  + lax.dot_general(rwh, lo, dn, preferred_element_type=F32)
                  + lax.dot_general(rwl, hi, dn, preferred_element_type=F32)) + rb_ref[0]

        tm = logits.shape[1]
        ie = lax.broadcasted_iota(I32, (N_EXPERTS, tm), 0)
        work = logits
        vals, idxs = [], []
        for _ in range(TOP_K):
            m = jnp.max(work, axis=0, keepdims=True)
            idx = jnp.min(jnp.where(work == m, ie, N_EXPERTS), axis=0, keepdims=True)
            vals.append(m)
            idxs.append(idx)
            work = jnp.where(ie == idx, -jnp.inf, work)
        ex = [jnp.exp(v - vals[0]) for v in vals]
        den = ex[0] + ex[1] + ex[2] + ex[3]
        ti_ref[...] = jnp.concatenate(idxs, axis=0)
        gates = jnp.concatenate([e / den for e in ex] + [jnp.zeros((128 - TOP_K, tm), F32)], axis=0)
        gcol_ref[...] = gates.T


def _outproj(ya, yc, ys, x, mod_rows, mix_w, ffn_w, w_bf16, rwt, rb, layer, n_prompt_seg, nb_pad):
    t = x.shape[0]
    depth = mix_w.shape[0]

    def mod_map(k):
        def f(i, kk):
            b = _batch_of_tile(i, TM_OUT, n_prompt_seg)
            return ((layer * nb_pad + b) * N_MOD + k, 0, 0)
        return f

    lvec = lambda: pl.BlockSpec((1, 1, D_MODEL), lambda i, k: (layer, 0, 0))
    return pl.pallas_call(
        _outproj_kernel,
        grid=(t // TM_OUT, D_MODEL // TK_OUT),
        in_specs=[
            pl.BlockSpec((TM_OUT, D_ATTN), lambda i, k: (i, 0)),
            pl.BlockSpec((TM_OUT, D_CONV), lambda i, k: (i, 0)),
            pl.BlockSpec((TM_OUT, D_SGU), lambda i, k: (i, 0)),
            pl.BlockSpec((TM_OUT, D_MODEL), lambda i, k: (i, 0)),
            lvec(),
            pl.BlockSpec((1, 1, D_MODEL), mod_map(2)),
            lvec(),
            pl.BlockSpec((1, 1, D_MODEL), mod_map(3)),
            pl.BlockSpec((1, 1, D_MODEL), mod_map(4)),
            pl.BlockSpec((1, TK_OUT, D_MODEL), lambda i, k: (layer, k, 0)),
            pl.BlockSpec((1, N_EXPERTS, D_MODEL), lambda i, k: (layer, 0, 0)),
            pl.BlockSpec((1, N_EXPERTS, 1), lambda i, k: (layer, 0, 0)),
        ],
        out_specs=[
            pl.BlockSpec((TM_OUT, D_MODEL), lambda i, k: (i, 0)),
            pl.BlockSpec((TM_OUT, HALF), lambda i, k: (i, 0)),
            pl.BlockSpec((TOP_K, TM_OUT), lambda i, k: (0, i)),
            pl.BlockSpec((TM_OUT, 128), lambda i, k: (i, 0)),
        ],
        out_shape=[
            jax.ShapeDtypeStruct((t, D_MODEL), F32),
            jax.ShapeDtypeStruct((t, HALF), U32),
            jax.ShapeDtypeStruct((TOP_K, t), I32),
            jax.ShapeDtypeStruct((t, 128), F32),
        ],
        scratch_shapes=[
            pltpu.VMEM((D_MODEL // TK_OUT, TM_OUT, TK_OUT), BF16),
            pltpu.VMEM((TM_OUT, D_MODEL), F32),
        ],
        compiler_params=_cparams(("arbitrary", "arbitrary")),
        name="outproj",
    )(ya, yc, ys, x, mix_w.reshape(depth, 1, D_MODEL), mod_rows, ffn_w.reshape(depth, 1, D_MODEL), mod_rows,
      mod_rows, w_bf16, rwt, rb.reshape(depth, N_EXPERTS, 1))


def _rank_kernel(ti_ref, tri_ref, rank_ref, cnt_ref, base_ref):
    i = pl.program_id(0)

    @pl.when(i == 0)
    def _():
        base_ref[...] = jnp.zeros_like(base_ref)

    e = ti_ref[...]
    tt = e.shape[1]
    ie = lax.broadcasted_iota(I32, (N_EXPERTS, tt), 0)
    pre = base_ref[:, 0:1]
    ranks = []
    for s in range(TOP_K):
        hit = ie == e[s:s + 1, :]
        hf = hit.astype(F32)
        earlier = jnp.dot(hf.astype(BF16), tri_ref[...], preferred_element_type=F32)
        ranks.append(jnp.sum(jnp.where(hit, pre + earlier, 0.0), axis=0, keepdims=True))
        pre = pre + jnp.sum(hf, axis=1, keepdims=True)
    rank_ref[...] = jnp.concatenate(ranks, axis=0).astype(I32)
    total = jnp.broadcast_to(pre, base_ref.shape)
    base_ref[...] = total
    cnt_ref[...] = total.astype(I32)


def _rank(ti, tri):
    t = ti.shape[1]
    return pl.pallas_call(
        _rank_kernel,
        grid=(t // TT_RANK,),
        in_specs=[
            pl.BlockSpec((TOP_K, TT_RANK), lambda i: (0, i)),
            pl.BlockSpec((TT_RANK, TT_RANK), lambda i: (0, 0)),
        ],
        out_specs=[
            pl.BlockSpec((TOP_K, TT_RANK), lambda i: (0, i)),
            pl.BlockSpec((N_EXPERTS, 128), lambda i: (0, 0)),
        ],
        out_shape=[
            jax.ShapeDtypeStruct((TOP_K, t), I32),
            jax.ShapeDtypeStruct((N_EXPERTS, 128), I32),
        ],
        scratch_shapes=[pltpu.VMEM((N_EXPERTS, 128), F32)],
        compiler_params=_cparams(("arbitrary",)),
        name="rank",
    )(ti, tri)


def _dispatch_kernel(fill_lo_ref, fill_hi_ref, dest_ref, hp_ref, xs_ref, zero_ref, row_sem, pad_sem):
    i = pl.program_id(0)
    n_fill = fill_lo_ref.shape[0]

    def pad_copy(p):
        return pltpu.make_async_copy(zero_ref.at[pl.ds(0, 1)], xs_ref.at[pl.ds(p, 1)], pad_sem)

    def row_copy(src, dst):
        return pltpu.make_async_copy(hp_ref.at[pl.ds(src, 1)], xs_ref.at[pl.ds(dst, 1)], row_sem)

    @pl.when(i == 0)
    def _():
        zero_ref[...] = jnp.zeros_like(zero_ref)
        for f in range(n_fill):
            lo = fill_lo_ref[f]
            hi = fill_hi_ref[f]

            def start(p, c):
                pad_copy(p).start()
                return c

            def wait(p, c):
                pad_copy(p).wait()
                return c

            lax.fori_loop(lo, hi, start, 0)
            lax.fori_loop(lo, hi, wait, 0)

    def issue(tk, c):
        for s in range(TOP_K):
            row_copy(i * TT_DISP + tk, dest_ref[s, tk]).start()
        return c

    def drain(tk, c):
        for s in range(TOP_K):
            row_copy(0, 0).wait()
        return c

    lax.fori_loop(0, TT_DISP, issue, 0)
    lax.fori_loop(0, TT_DISP, drain, 0)


def _dispatch(fill_lo, fill_hi, dest, hp, n_pad):
    t = hp.shape[0]
    grid_spec = pltpu.PrefetchScalarGridSpec(
        num_scalar_prefetch=2,
        grid=(t // TT_DISP,),
        in_specs=[
            pl.BlockSpec((TOP_K, TT_DISP), lambda i, lo, hi: (0, i), memory_space=pltpu.SMEM),
            pl.BlockSpec(memory_space=pl.ANY),
        ],
        out_specs=pl.BlockSpec(memory_space=pl.ANY),
        scratch_shapes=[
            pltpu.VMEM((8, HALF), U32),
            pltpu.SemaphoreType.DMA(()),
            pltpu.SemaphoreType.DMA(()),
        ],
    )
    return pl.pallas_call(
        _dispatch_kernel,
        grid_spec=grid_spec,
        out_shape=jax.ShapeDtypeStruct((n_pad, HALF), U32),
        compiler_params=_cparams(("arbitrary",)),
        name="dispatch",
    )(fill_lo, fill_hi, dest, hp)


def _expert_kernel(be_ref, nu_ref, x_ref, wg_ref, wu_ref, bg_ref, bu_ref, wd_ref, bd_ref, y_ref, xb_ref, acc_ref):
    i = pl.program_id(0)
    j = pl.program_id(1)
    nf = D_FF // TF_EXP
    active = i < nu_ref[0]

    @pl.when(jnp.logical_and(active, j == 0))
    def _():
        u = x_ref[...]
        xb_ref[:, :HALF] = _unpack_lo(u).astype(BF16)
        xb_ref[:, HALF:] = _unpack_hi(u).astype(BF16)
        acc_ref[...] = jnp.zeros_like(acc_ref)

    @pl.when(active)
    def _():
        xb = xb_ref[...]
        gate = jnp.dot(xb, wg_ref[0, 0].astype(BF16), preferred_element_type=F32) + bg_ref[0, 0]
        up = jnp.dot(xb, wu_ref[0, 0].astype(BF16), preferred_element_type=F32) + bu_ref[0, 0]
        gate = jnp.minimum(gate, SWIGLU_LIMIT)
        up = jnp.clip(up, -SWIGLU_LIMIT, SWIGLU_LIMIT)
        act = (up + 1.0) * (gate * jax.nn.sigmoid(SWIGLU_ALPHA * gate))
        acc_ref[...] += jnp.dot(act.astype(BF16), wd_ref[0, 0].astype(BF16), preferred_element_type=F32)

    @pl.when(jnp.logical_and(active, j == nf - 1))
    def _():
        y_ref[...] = _pack_bf16_pairs(acc_ref[...] + bd_ref[0, 0])

    @pl.when(jnp.logical_and(jnp.logical_not(active), j == nf - 1))
    def _():
        y_ref[...] = jnp.zeros_like(y_ref)


def _experts(blk_e, n_used, xs, w_gu, b_gu, w_dn, b_dn, layer):
    n_pad = xs.shape[0]
    depth = w_gu.shape[0]
    nf = D_FF // TF_EXP

    def blk(i, nu):
        return jnp.minimum(i, nu[0] - 1)

    def ftile(i, j, nu):
        return jnp.where(i < nu[0], j, nf - 1)

    grid_spec = pltpu.PrefetchScalarGridSpec(
        num_scalar_prefetch=2,
        grid=(n_pad // TM_EXP, nf),
        in_specs=[
            pl.BlockSpec((TM_EXP, HALF), lambda i, j, be, nu: (blk(i, nu), 0)),
            pl.BlockSpec((1, 1, D_MODEL, TF_EXP), lambda i, j, be, nu: (layer, be[blk(i, nu)], 0, ftile(i, j, nu))),
            pl.BlockSpec((1, 1, D_MODEL, TF_EXP),
                         lambda i, j, be, nu: (layer, be[blk(i, nu)], 0, nf + ftile(i, j, nu))),
            pl.BlockSpec((1, 1, 1, TF_EXP), lambda i, j, be, nu: (layer, be[blk(i, nu)], 0, ftile(i, j, nu))),
            pl.BlockSpec((1, 1, 1, TF_EXP), lambda i, j, be, nu: (layer, be[blk(i, nu)], 0, nf + ftile(i, j, nu))),
            pl.BlockSpec((1, 1, TF_EXP, D_MODEL), lambda i, j, be, nu: (layer, be[blk(i, nu)], ftile(i, j, nu), 0)),
            pl.BlockSpec((1, 1, 1, D_MODEL), lambda i, j, be, nu: (layer, be[blk(i, nu)], 0, 0)),
        ],
        out_specs=pl.BlockSpec((TM_EXP, HALF), lambda i, j, be, nu: (i, 0)),
        scratch_shapes=[
            pltpu.VMEM((TM_EXP, D_MODEL), BF16),
            pltpu.VMEM((TM_EXP, D_MODEL), F32),
        ],
    )
    return pl.pallas_call(
        _expert_kernel,
        grid_spec=grid_spec,
        out_shape=jax.ShapeDtypeStruct((n_pad, HALF), U32),
        compiler_params=_cparams(("arbitrary", "arbitrary")),
        name="experts",
    )(blk_e, n_used, xs, w_gu, w_gu, b_gu.reshape(depth, N_EXPERTS, 1, 2 * D_FF),
      b_gu.reshape(depth, N_EXPERTS, 1, 2 * D_FF), w_dn, b_dn.reshape(depth, N_EXPERTS, 1, D_MODEL))


def _combine_kernel(dest_ref, gcol_ref, x_ref, g2_ref, y_ref, o_ref, buf_ref, sem):
    def row_copy(s, tk, src):
        return pltpu.make_async_copy(y_ref.at[pl.ds(src, 1)], buf_ref.at[s, pl.ds(tk, 1)], sem)

    def issue(tk, c):
        for s in range(TOP_K):
            row_copy(s, tk, dest_ref[s, tk]).start()
        return c

    def drain(tk, c):
        for s in range(TOP_K):
            row_copy(s, tk, 0).wait()
        return c

    lax.fori_loop(0, TT_COMB, issue, 0)
    lax.fori_loop(0, TT_COMB, drain, 0)

    g = gcol_ref[...]
    lo = jnp.zeros((TT_COMB, HALF), F32)
    hi = jnp.zeros((TT_COMB, HALF), F32)
    for s in range(TOP_K):
        u = buf_ref[s]
        gs = g[:, s:s + 1]
        lo = lo + _unpack_lo(u) * gs
        hi = hi + _unpack_hi(u) * gs
    g2 = g2_ref[0]
    o_ref[:, :HALF] = x_ref[:, :HALF] + g2[:, :HALF] * lo
    o_ref[:, HALF:] = x_ref[:, HALF:] + g2[:, HALF:] * hi


def _combine(dest, gcol, x1, mod_rows, y, layer, n_prompt_seg, nb_pad):
    t = x1.shape[0]

    def g2_map(i):
        b = _batch_of_tile(i, TT_COMB, n_prompt_seg)
        return ((layer * nb_pad + b) * N_MOD + 5, 0, 0)

    return pl.pallas_call(
        _combine_kernel,
        grid=(t // TT_COMB,),
        in_specs=[
            pl.BlockSpec((TOP_K, TT_COMB), lambda i: (0, i), memory_space=pltpu.SMEM),
            pl.BlockSpec((TT_COMB, 128), lambda i: (i, 0)),
            pl.BlockSpec((TT_COMB, D_MODEL), lambda i: (i, 0)),
            pl.BlockSpec((1, 1, D_MODEL), g2_map),
            pl.BlockSpec(memory_space=pl.ANY),
        ],
        out_specs=pl.BlockSpec((TT_COMB, D_MODEL), lambda i: (i, 0)),
        out_shape=jax.ShapeDtypeStruct((t, D_MODEL), F32),
        scratch_shapes=[
            pltpu.VMEM((TOP_K, TT_COMB, HALF), U32),
            pltpu.SemaphoreType.DMA(()),
        ],
        compiler_params=_cparams(("arbitrary",)),
        name="combine",
    )(dest, gcol, x1, mod_rows, y)


def _routing_plan(ti, rank, counts, n_blocks):
    pcounts = (counts + TM_EXP - 1) // TM_EXP * TM_EXP
    pend = jnp.cumsum(pcounts)
    pstart = pend - pcounts
    dest = pstart[ti] + rank
    blk_e = jnp.minimum(
        jnp.searchsorted(pend, jnp.arange(n_blocks, dtype=I32) * TM_EXP, side='right'), N_EXPERTS - 1).astype(I32)
    n_used = (pend[-1:] // TM_EXP).astype(I32)
    fill_lo = jnp.concatenate([pstart + counts, pend[-1:]]).astype(I32)
    fill_hi = jnp.concatenate([pend, jnp.full((1,), n_blocks * TM_EXP, I32)]).astype(I32)
    return dest.astype(I32), blk_e, n_used, fill_lo, fill_hi


def kernel(x_prompt, x_sample, c_prompt, c_sample, ada_w, ada_b, norm_mix_w, norm_ffn_w, w_in, q_norm_w, k_norm_w, rel_pos_bias, conv_w, conv_b, conv_norm_w, conv_norm_b, sgu_norm_w, sgu_norm_b, sgu_w, sgu_b, mix_norm_w, w_out, router_w, router_b, w_gate_up, b_gate_up, w_down, b_down):
    bp, n_p, d = x_prompt.shape
    bs, n_s, _ = x_sample.shape
    assert d == D_MODEL and n_p == 2 * SEG and n_s == SEG
    depth = ada_w.shape[0]
    n_prompt_seg = 2 * bp
    t = bp * n_p + bs * n_s
    nb = bp + bs
    nb_pad = -(-nb // 8) * 8

    x = jnp.concatenate([x_prompt.reshape(bp * n_p, d), x_sample.reshape(bs * n_s, d)], axis=0)
    c_pad = jnp.concatenate([c_prompt, c_sample, jnp.zeros((nb_pad - nb, d), F32)], axis=0)
    mod_rows = _adaln(c_pad, ada_w, ada_b).reshape(depth * nb_pad * N_MOD, 1, D_MODEL)

    w_in_b = w_in.astype(BF16)
    w_out_b = w_out.astype(BF16)
    rwt = jnp.swapaxes(router_w, 1, 2)
    bias_tab = _bias_table(rel_pos_bias)
    group = np.arange(TN_IN) // ATTN_HD
    ones_bd = jnp.asarray(group[:, None] == group[None, :], BF16)
    tri = jnp.asarray(np.arange(TT_RANK)[:, None] < np.arange(TT_RANK)[None, :], BF16)
    n_slot = t * TOP_K
    n_blocks = -(-(n_slot + N_EXPERTS * (TM_EXP - 1)) // TM_EXP)

    for l in range(depth):
        qn = jnp.tile(q_norm_w[l] * (ATTN_HD ** -0.5), TN_IN // ATTN_HD).reshape(1, TN_IN)
        kn = jnp.tile(k_norm_w[l], TN_IN // ATTN_HD).reshape(1, TN_IN)
        sgu_bias = jnp.repeat(sgu_b[l].T, D_SGU // SGU_GROUPS, axis=1)

        p = _inproj(x, mod_rows, norm_mix_w, w_in_b, qn, kn, ones_bd, l, n_prompt_seg, nb_pad)
        ya = _attention(p, bias_tab, l, n_prompt_seg)
        yc = _conv(p, conv_w, conv_b, conv_norm_w, conv_norm_b, l, n_prompt_seg)
        ys = _sgu(p, sgu_norm_w, sgu_norm_b, sgu_w, sgu_bias, l)
        x1, hp, ti, gcol = _outproj(ya, yc, ys, x, mod_rows, mix_norm_w, norm_ffn_w, w_out_b, rwt, router_b, l,
                                    n_prompt_seg, nb_pad)
        rank, cnt = _rank(ti, tri)
        dest, blk_e, n_used, fill_lo, fill_hi = _routing_plan(ti, rank, cnt[:, 0], n_blocks)
        xs = _dispatch(fill_lo, fill_hi, dest, hp, n_blocks * TM_EXP)
        y = _experts(blk_e, n_used, xs, w_gate_up, b_gate_up, w_down, b_down, l)
        x = _combine(dest, gcol, x1, mod_rows, y, l, n_prompt_seg, nb_pad)

    y_prompt = x[:bp * n_p].reshape(bp, n_p, d)
    y_sample = x[bp * n_p:].reshape(bs, n_s, d)
    return (y_prompt, y_sample)
```

```python
import functools

import numpy as np
import jax
import jax.numpy as jnp
from jax import lax
from jax.experimental import pallas as pl
from jax.experimental.pallas import tpu as pltpu

F32 = jnp.float32
BF16 = jnp.bfloat16
U32 = jnp.uint32
I32 = jnp.int32

D_MODEL = 2048
GRID_W = 64
ATTN_HEADS = 16
ATTN_HD = 64
D_ATTN = ATTN_HEADS * ATTN_HD
WIN_R = 8
WIN_C = 16
D_CONV = 512
CONV_K = 31
D_SGU = 512
SGU_GROUPS = 8
SGU_CHUNK = 128
D_IN = 3 * D_ATTN + 2 * D_CONV + 2 * D_SGU
N_EXPERTS = 32
TOP_K = 4
D_FF = 2048
SWIGLU_LIMIT = 7.0
SWIGLU_ALPHA = 1.702
N_MOD = 6
EPS = 1e-6

SEG = 2048
SEG_ROWS = SEG // GRID_W
HALF = D_MODEL // 2
NEG_BIAS = -1e30
HI_MASK = 0xFFFF0000

V7X_VMEM_LIMIT_BYTES = 56 * 1024 * 1024

TN_ADA = 1024
TM_IN, TN_IN = 1024, 512
TM_SGU = 1024
TM_OUT, TK_OUT = 512, 512
TT_RANK = 512
TT_DISP = 512
TM_EXP, TF_EXP = 1024, 256
TT_COMB = 256
ATTN_ROW_GROUP = 4
CONV_ROWS = 32
HALO = 16


def _cparams(sem):
    return pltpu.CompilerParams(dimension_semantics=sem, vmem_limit_bytes=V7X_VMEM_LIMIT_BYTES)


def _batch_of_tile(i, tile, n_prompt_seg):
    seg = (i * tile) // SEG
    return jnp.where(seg < n_prompt_seg, seg // 2, seg - n_prompt_seg // 2)


def _pack_bf16_pairs(y):
    yb = y.astype(BF16)
    lo = pltpu.bitcast(yb[:, :HALF].astype(F32), U32) >> 16
    hi = pltpu.bitcast(yb[:, HALF:].astype(F32), U32) & jnp.uint32(HI_MASK)
    return lo | hi


def _unpack_lo(u):
    return pltpu.bitcast(u << 16, F32)


def _unpack_hi(u):
    return pltpu.bitcast(u & jnp.uint32(HI_MASK), F32)


def _adaln_kernel(c_ref, w_ref, b_ref, o_ref):
    sc = jax.nn.silu(c_ref[...]).astype(BF16)
    o_ref[0] = jnp.dot(sc, w_ref[0].astype(BF16), preferred_element_type=F32) + b_ref[0]


def _adaln(c_pad, ada_w, ada_b):
    depth = ada_w.shape[0]
    nb = c_pad.shape[0]
    n_out = N_MOD * D_MODEL
    return pl.pallas_call(
        _adaln_kernel,
        grid=(depth, n_out // TN_ADA),
        in_specs=[
            pl.BlockSpec((nb, D_MODEL), lambda l, j: (0, 0)),
            pl.BlockSpec((1, D_MODEL, TN_ADA), lambda l, j: (l, 0, j)),
            pl.BlockSpec((1, 1, TN_ADA), lambda l, j: (l, 0, j)),
        ],
        out_specs=pl.BlockSpec((1, nb, TN_ADA), lambda l, j: (l, 0, j)),
        out_shape=jax.ShapeDtypeStruct((depth, nb, n_out), F32),
        compiler_params=_cparams(("arbitrary", "arbitrary")),
        name="adaln",
    )(c_pad, ada_w, ada_b.reshape(depth, 1, n_out))


def _inproj_kernel(x_ref, nw_ref, shift_ref, scale_ref, w_ref, qn_ref, kn_ref, ones_ref, o_ref, h_ref):
    j = pl.program_id(1)

    @pl.when(j == 0)
    def _():
        xf = x_ref[...]
        y = xf * lax.rsqrt(jnp.mean(xf * xf, axis=-1, keepdims=True) + EPS)
        h = (y * nw_ref[0]) * (1.0 + scale_ref[0]) + shift_ref[0]
        h_ref[...] = h.astype(BF16)

    acc = jnp.dot(h_ref[...], w_ref[0], preferred_element_type=F32)
    n_qk = 2 * D_ATTN // TN_IN

    @pl.when(j < n_qk)
    def _():
        ss = jnp.dot((acc * acc).astype(BF16), ones_ref[...], preferred_element_type=F32)
        r = lax.rsqrt(ss * (1.0 / ATTN_HD) + EPS)
        nw = jnp.where(j < n_qk // 2, qn_ref[...], kn_ref[...])
        o_ref[...] = (acc * r * nw).astype(o_ref.dtype)

    @pl.when(j >= n_qk)
    def _():
        o_ref[...] = acc.astype(o_ref.dtype)


def _inproj(x, mod_rows, norm_w, w_bf16, qn, kn, ones_bd, layer, n_prompt_seg, nb_pad):
    t = x.shape[0]
    depth = norm_w.shape[0]

    def mod_map(k):
        def f(i, j):
            b = _batch_of_tile(i, TM_IN, n_prompt_seg)
            return ((layer * nb_pad + b) * N_MOD + k, 0, 0)
        return f

    return pl.pallas_call(
        _inproj_kernel,
        grid=(t // TM_IN, D_IN // TN_IN),
        in_specs=[
            pl.BlockSpec((TM_IN, D_MODEL), lambda i, j: (i, 0)),
            pl.BlockSpec((1, 1, D_MODEL), lambda i, j: (layer, 0, 0)),
            pl.BlockSpec((1, 1, D_MODEL), mod_map(0)),
            pl.BlockSpec((1, 1, D_MODEL), mod_map(1)),
            pl.BlockSpec((1, D_MODEL, TN_IN), lambda i, j: (layer, 0, j)),
            pl.BlockSpec((1, TN_IN), lambda i, j: (0, 0)),
            pl.BlockSpec((1, TN_IN), lambda i, j: (0, 0)),
            pl.BlockSpec((TN_IN, TN_IN), lambda i, j: (0, 0)),
        ],
        out_specs=pl.BlockSpec((TM_IN, TN_IN), lambda i, j: (i, j)),
        out_shape=jax.ShapeDtypeStruct((t, D_IN), BF16),
        scratch_shapes=[pltpu.VMEM((TM_IN, D_MODEL), BF16)],
        compiler_params=_cparams(("arbitrary", "arbitrary")),
        name="inproj",
    )(x, norm_w.reshape(depth, 1, D_MODEL), mod_rows, mod_rows, w_bf16, qn, kn, ones_bd)


def _attn_kernel(q_ref, k_ref, v_ref, b_ref, o_ref, *, n_prompt_seg):
    seg = pl.program_id(0)
    is_prompt = seg < n_prompt_seg
    half = seg % 2
    rows = jnp.where(is_prompt, 2 * SEG_ROWS, SEG_ROWS)
    row0 = jnp.where(is_prompt, half * SEG_ROWS, 0)
    kv0 = jnp.where(is_prompt, 0, half * SEG)
    lane = lax.broadcasted_iota(I32, (GRID_W, 2 * ATTN_HD), 1)
    first_head = lane < ATTN_HD
    band = WIN_R * GRID_W

    def group_body(gi, carry):
        q0s, k0s, scores = [], [], []
        for g in range(ATTN_ROW_GROUP):
            rr = gi * ATTN_ROW_GROUP + g
            r = row0 + rr
            rs = jnp.clip(r - WIN_R // 2, 0, rows - WIN_R)
            di = r - rs
            q0 = pl.multiple_of(rr * GRID_W, GRID_W)
            k0 = pl.multiple_of(kv0 + rs * GRID_W, GRID_W)
            q = q_ref[pl.ds(q0, GRID_W), :]
            zero = jnp.zeros_like(q)
            qm = jnp.concatenate([jnp.where(first_head, q, zero), jnp.where(first_head, zero, q)], axis=0)
            s = lax.dot_general(qm, k_ref[pl.ds(k0, band), :], (((1,), (1,)), ((), ())),
                                preferred_element_type=F32)
            scores.append(s + b_ref[0, 0, di])
            q0s.append(q0)
            k0s.append(k0)
        probs, inv = [], []
        for s in scores:
            m = jnp.max(s, axis=-1, keepdims=True)
            p = jnp.exp(s - m)
            inv.append(1.0 / jnp.sum(p, axis=-1, keepdims=True))
            probs.append(p.astype(BF16))
        for g in range(ATTN_ROW_GROUP):
            o = jnp.dot(probs[g], v_ref[pl.ds(k0s[g], band), :], preferred_element_type=F32) * inv[g]
            o_ref[pl.ds(q0s[g], GRID_W), :] = jnp.where(first_head, o[:GRID_W], o[GRID_W:]).astype(o_ref.dtype)
        return carry

    lax.fori_loop(0, SEG_ROWS // ATTN_ROW_GROUP, group_body, 0)


def _attention(p, bias_tab, layer, n_prompt_seg):
    t = p.shape[0]
    n_seg = t // SEG
    n_pair = ATTN_HEADS // 2
    lanes = 2 * ATTN_HD
    return pl.pallas_call(
        functools.partial(_attn_kernel, n_prompt_seg=n_prompt_seg),
        grid=(n_seg, n_pair),
        in_specs=[
            pl.BlockSpec((SEG, lanes), lambda s, h: (s, h)),
            pl.BlockSpec((2 * SEG, lanes), lambda s, h: (s // 2, n_pair + h)),
            pl.BlockSpec((2 * SEG, lanes), lambda s, h: (s // 2, 2 * n_pair + h)),
            pl.BlockSpec((1, 1, WIN_R, 2 * GRID_W, WIN_R * GRID_W), lambda s, h: (layer, h, 0, 0, 0)),
        ],
        out_specs=pl.BlockSpec((SEG, lanes), lambda s, h: (s, h)),
        out_shape=jax.ShapeDtypeStruct((t, D_ATTN), BF16),
        compiler_params=_cparams(("arbitrary", "arbitrary")),
        name="attention",
    )(p, p, p, bias_tab)


def _bias_table(rel_pos_bias):
    c = np.arange(GRID_W)
    cs = np.clip(c - WIN_C // 2, 0, GRID_W - WIN_C)
    kc = np.arange(GRID_W)
    valid = (kc[None, :] >= cs[:, None]) & (kc[None, :] < cs[:, None] + WIN_C)
    col_rel = np.clip(kc[None, :] - c[:, None] + (WIN_C - 1), 0, 2 * WIN_C - 2)
    depth = rel_pos_bias.shape[0]
    n_col = 2 * WIN_C - 1
    onehot = jnp.asarray(np.arange(n_col)[:, None] == col_rel.reshape(1, -1), F32)
    cols = jnp.einsum('lhrx,xk->lhrk', rel_pos_bias.astype(F32), onehot, precision=lax.Precision.HIGHEST)
    cols = jnp.where(valid.reshape(1, 1, 1, -1), cols, NEG_BIAS)
    cols = cols.reshape(depth, ATTN_HEADS, 2 * WIN_R - 1, GRID_W, GRID_W)
    tab = jnp.stack([cols[:, :, WIN_R - 1 - di:2 * WIN_R - 1 - di] for di in range(WIN_R)], axis=2)
    tab = jnp.swapaxes(tab, 3, 4).reshape(depth, ATTN_HEADS // 2, 2, WIN_R, GRID_W, WIN_R * GRID_W)
    return jnp.swapaxes(tab, 2, 3).reshape(depth, ATTN_HEADS // 2, WIN_R, 2 * GRID_W, WIN_R * GRID_W)


def _conv_kernel(am_ref, gm_ref, ap_ref, gp_ref, an_ref, gn_ref, w_ref, cb_ref, lw_ref, lb_ref, o_ref, buf_ref,
                 *, n_prompt_seg):
    seg = pl.program_id(0)
    is_prompt = seg < n_prompt_seg
    half = seg % 2
    has_prev = jnp.logical_and(is_prompt, half == 1)
    has_next = jnp.logical_and(is_prompt, half == 0)

    def glu(a_ref, g_ref):
        return a_ref[...].astype(F32) * jax.nn.sigmoid(g_ref[...].astype(F32))

    buf_ref[pl.ds(HALO, SEG), :] = glu(am_ref, gm_ref)
    buf_ref[pl.ds(0, HALO), :] = jnp.where(has_prev, glu(ap_ref, gp_ref), 0.0)
    buf_ref[pl.ds(HALO + SEG, HALO), :] = jnp.where(has_next, glu(an_ref, gn_ref), 0.0)

    w = w_ref[0]
    off = HALO - CONV_K // 2

    def chunk(ci, carry):
        r0 = pl.multiple_of(ci * CONV_ROWS, CONV_ROWS)
        win_rows = CONV_ROWS + 32
        win = buf_ref[pl.ds(r0, win_rows), :]
        acc = jnp.zeros((CONV_ROWS, D_CONV), F32)
        for s in range(8):
            ws = win if s == 0 else pltpu.roll(win, win_rows - s, axis=0)
            for a in range(4):
                k = 8 * a + s - off
                if 0 <= k < CONV_K:
                    acc = acc + ws[8 * a:8 * a + CONV_ROWS, :] * w[k:k + 1, :]
        y = acc + cb_ref[0]
        mu = jnp.mean(y, axis=-1, keepdims=True)
        yc = y - mu
        var = jnp.mean(yc * yc, axis=-1, keepdims=True)
        yn = yc * lax.rsqrt(var + EPS) * lw_ref[0] + lb_ref[0]
        o_ref[pl.ds(r0, CONV_ROWS), :] = jax.nn.silu(yn).astype(o_ref.dtype)
        return carry

    lax.fori_loop(0, SEG // CONV_ROWS, chunk, 0)


def _conv(p, conv_w, conv_b, norm_w, norm_b, layer, n_prompt_seg):
    t = p.shape[0]
    depth = conv_w.shape[0]
    n_seg = t // SEG
    ca = 3 * D_ATTN // D_CONV
    cg = ca + 1
    per_seg = SEG // HALO
    last = t // HALO - 1

    def prev_map(col):
        return lambda s: (jnp.maximum(s * per_seg - 1, 0), col)

    def next_map(col):
        return lambda s: (jnp.minimum((s + 1) * per_seg, last), col)

    vec = lambda: pl.BlockSpec((1, 1, D_CONV), lambda s: (layer, 0, 0))
    return pl.pallas_call(
        functools.partial(_conv_kernel, n_prompt_seg=n_prompt_seg),
        grid=(n_seg,),
        in_specs=[
            pl.BlockSpec((SEG, D_CONV), lambda s: (s, ca)),
            pl.BlockSpec((SEG, D_CONV), lambda s: (s, cg)),
            pl.BlockSpec((HALO, D_CONV), prev_map(ca)),
            pl.BlockSpec((HALO, D_CONV), prev_map(cg)),
            pl.BlockSpec((HALO, D_CONV), next_map(ca)),
            pl.BlockSpec((HALO, D_CONV), next_map(cg)),
            pl.BlockSpec((1, CONV_K, D_CONV), lambda s: (layer, 0, 0)),
            vec(), vec(), vec(),
        ],
        out_specs=pl.BlockSpec((SEG, D_CONV), lambda s: (s, 0)),
        out_shape=jax.ShapeDtypeStruct((t, D_CONV), BF16),
        scratch_shapes=[pltpu.VMEM((SEG + 2 * HALO, D_CONV), F32)],
        compiler_params=_cparams(("arbitrary",)),
        name="conv",
    )(p, p, p, p, p, p, conv_w, conv_b.reshape(depth, 1, D_CONV), norm_w.reshape(depth, 1, D_CONV),
      norm_b.reshape(depth, 1, D_CONV))


def _sgu_kernel(u_ref, v_ref, lw_ref, lb_ref, ws_ref, bs_ref, o_ref):
    u = jax.nn.gelu(u_ref[...].astype(F32))
    g = jax.nn.gelu(v_ref[...].astype(F32))
    mu = jnp.mean(g, axis=-1, keepdims=True)
    gc = g - mu
    var = jnp.mean(gc * gc, axis=-1, keepdims=True)
    v = (gc * lax.rsqrt(var + EPS) * lw_ref[0] + lb_ref[0]).astype(BF16)
    lanes = 2 * (D_SGU // SGU_GROUPS)
    first_group = lax.broadcasted_iota(I32, (SGU_CHUNK, lanes), 1) < lanes // 2
    for c in range(TM_SGU // SGU_CHUNK):
        rows = slice(c * SGU_CHUNK, (c + 1) * SGU_CHUNK)
        for pair in range(SGU_GROUPS // 2):
            cols = slice(pair * lanes, (pair + 1) * lanes)
            vp = v[rows, cols]
            m0 = jnp.dot(ws_ref[0, 2 * pair].astype(BF16), vp, preferred_element_type=F32)
            m1 = jnp.dot(ws_ref[0, 2 * pair + 1].astype(BF16), vp, preferred_element_type=F32)
            mixed = jnp.where(first_group, m0, m1) + bs_ref[:, cols]
            o_ref[rows, cols] = (u[rows, cols] * mixed).astype(o_ref.dtype)


def _sgu(p, norm_w, norm_b, sgu_w, bias_exp, layer):
    t = p.shape[0]
    depth = norm_w.shape[0]
    su = (3 * D_ATTN + 2 * D_CONV) // D_SGU
    sv = su + 1
    vec = lambda: pl.BlockSpec((1, 1, D_SGU), lambda i: (layer, 0, 0))
    return pl.pallas_call(
        _sgu_kernel,
        grid=(t // TM_SGU,),
        in_specs=[
            pl.BlockSpec((TM_SGU, D_SGU), lambda i: (i, su)),
            pl.BlockSpec((TM_SGU, D_SGU), lambda i: (i, sv)),
            vec(), vec(),
            pl.BlockSpec((1, SGU_GROUPS, SGU_CHUNK, SGU_CHUNK), lambda i: (layer, 0, 0, 0)),
            pl.BlockSpec((SGU_CHUNK, D_SGU), lambda i: (0, 0)),
        ],
        out_specs=pl.BlockSpec((TM_SGU, D_SGU), lambda i: (i, 0)),
        out_shape=jax.ShapeDtypeStruct((t, D_SGU), BF16),
        compiler_params=_cparams(("arbitrary",)),
        name="sgu",
    )(p, p, norm_w.reshape(depth, 1, D_SGU), norm_b.reshape(depth, 1, D_SGU), sgu_w, bias_exp)


def _rms(y, w):
    return y * lax.rsqrt(jnp.mean(y * y, axis=-1, keepdims=True) + EPS) * w


def _outproj_kernel(ya_ref, yc_ref, ys_ref, x_ref, mw_ref, g1_ref, nw_ref, sh2_ref, sc2_ref, w_ref, rwt_ref, rb_ref,
                    x1_ref, hp_ref, ti_ref, gcol_ref, mix_ref, acc_ref):
    k = pl.program_id(1)
    nk = D_MODEL // TK_OUT

    @pl.when(k == 0)
    def _():
        mw = mw_ref[0]
        na = _rms(ya_ref[...].astype(F32), mw[:, :D_ATTN])
        nc = _rms(yc_ref[...].astype(F32), mw[:, D_ATTN:D_ATTN + D_CONV])
        ns = _rms(ys_ref[...].astype(F32), mw[:, D_ATTN + D_CONV:])
        mix_ref[0] = na[:, :TK_OUT].astype(BF16)
        mix_ref[1] = na[:, TK_OUT:].astype(BF16)
        mix_ref[2] = nc.astype(BF16)
        mix_ref[3] = ns.astype(BF16)
        acc_ref[...] = jnp.zeros_like(acc_ref)

    acc_ref[...] += jnp.dot(mix_ref[k], w_ref[0], preferred_element_type=F32)

    @pl.when(k == nk - 1)
    def _():
        x1 = x_ref[...] + g1_ref[0] * acc_ref[...]
        x1_ref[...] = x1
        h2 = _rms(x1, nw_ref[0]) * (1.0 + sc2_ref[0]) + sh2_ref[0]
        hp_ref[...] = _pack_bf16_pairs(h2)

        hi = h2.astype(BF16)
        lo = (h2 - hi.astype(F32)).astype(BF16)
        rw = rwt_ref[0]
        rwh = rw.astype(BF16)
        rwl = (rw - rwh.astype(F32)).astype(BF16)
        dn = (((1,), (1,)), ((), ()))
        logits = (lax.dot_general(rwh, hi, dn, preferred_element_type=F32)
                  + lax.dot_general(rwh, lo, dn, preferred_element_type=F32)
                  + lax.dot_general(rwl, hi, dn, preferred_element_type=F32)) + rb_ref[0]

        tm = logits.shape[1]
        ie = lax.broadcasted_iota(I32, (N_EXPERTS, tm), 0)
        work = logits
        vals, idxs = [], []
        for _ in range(TOP_K):
            m = jnp.max(work, axis=0, keepdims=True)
            idx = jnp.min(jnp.where(work == m, ie, N_EXPERTS), axis=0, keepdims=True)
            vals.append(m)
            idxs.append(idx)
            work = jnp.where(ie == idx, -jnp.inf, work)
        ex = [jnp.exp(v - vals[0]) for v in vals]
        den = ex[0] + ex[1] + ex[2] + ex[3]
        ti_ref[...] = jnp.concatenate(idxs, axis=0)
        gates = jnp.concatenate([e / den for e in ex] + [jnp.zeros((128 - TOP_K, tm), F32)], axis=0)
        gcol_ref[...] = gates.T


def _outproj(ya, yc, ys, x, mod_rows, mix_w, ffn_w, w_bf16, rwt, rb, layer, n_prompt_seg, nb_pad):
    t = x.shape[0]
    depth = mix_w.shape[0]

    def mod_map(k):
        def f(i, kk):
            b = _batch_of_tile(i, TM_OUT, n_prompt_seg)
            return ((layer * nb_pad + b) * N_MOD + k, 0, 0)
        return f

    lvec = lambda: pl.BlockSpec((1, 1, D_MODEL), lambda i, k: (layer, 0, 0))
    return pl.pallas_call(
        _outproj_kernel,
        grid=(t // TM_OUT, D_MODEL // TK_OUT),
        in_specs=[
            pl.BlockSpec((TM_OUT, D_ATTN), lambda i, k: (i, 0)),
            pl.BlockSpec((TM_OUT, D_CONV), lambda i, k: (i, 0)),
            pl.BlockSpec((TM_OUT, D_SGU), lambda i, k: (i, 0)),
            pl.BlockSpec((TM_OUT, D_MODEL), lambda i, k: (i, 0)),
            lvec(),
            pl.BlockSpec((1, 1, D_MODEL), mod_map(2)),
            lvec(),
            pl.BlockSpec((1, 1, D_MODEL), mod_map(3)),
            pl.BlockSpec((1, 1, D_MODEL), mod_map(4)),
            pl.BlockSpec((1, TK_OUT, D_MODEL), lambda i, k: (layer, k, 0)),
            pl.BlockSpec((1, N_EXPERTS, D_MODEL), lambda i, k: (layer, 0, 0)),
            pl.BlockSpec((1, N_EXPERTS, 1), lambda i, k: (layer, 0, 0)),
        ],
        out_specs=[
            pl.BlockSpec((TM_OUT, D_MODEL), lambda i, k: (i, 0)),
            pl.BlockSpec((TM_OUT, HALF), lambda i, k: (i, 0)),
            pl.BlockSpec((TOP_K, TM_OUT), lambda i, k: (0, i)),
            pl.BlockSpec((TM_OUT, 128), lambda i, k: (i, 0)),
        ],
        out_shape=[
            jax.ShapeDtypeStruct((t, D_MODEL), F32),
            jax.ShapeDtypeStruct((t, HALF), U32),
            jax.ShapeDtypeStruct((TOP_K, t), I32),
            jax.ShapeDtypeStruct((t, 128), F32),
        ],
        scratch_shapes=[
            pltpu.VMEM((D_MODEL // TK_OUT, TM_OUT, TK_OUT), BF16),
            pltpu.VMEM((TM_OUT, D_MODEL), F32),
        ],
        compiler_params=_cparams(("arbitrary", "arbitrary")),
        name="outproj",
    )(ya, yc, ys, x, mix_w.reshape(depth, 1, D_MODEL), mod_rows, ffn_w.reshape(depth, 1, D_MODEL), mod_rows,
      mod_rows, w_bf16, rwt, rb.reshape(depth, N_EXPERTS, 1))


def _rank_kernel(ti_ref, tri_ref, rank_ref, cnt_ref, base_ref):
    i = pl.program_id(0)

    @pl.when(i == 0)
    def _():
        base_ref[...] = jnp.zeros_like(base_ref)

    e = ti_ref[...]
    tt = e.shape[1]
    ie = lax.broadcasted_iota(I32, (N_EXPERTS, tt), 0)
    pre = base_ref[:, 0:1]
    ranks = []
    for s in range(TOP_K):
        hit = ie == e[s:s + 1, :]
        hf = hit.astype(F32)
        earlier = jnp.dot(hf.astype(BF16), tri_ref[...], preferred_element_type=F32)
        ranks.append(jnp.sum(jnp.where(hit, pre + earlier, 0.0), axis=0, keepdims=True))
        pre = pre + jnp.sum(hf, axis=1, keepdims=True)
    rank_ref[...] = jnp.concatenate(ranks, axis=0).astype(I32)
    total = jnp.broadcast_to(pre, base_ref.shape)
    base_ref[...] = total
    cnt_ref[...] = total.astype(I32)


def _rank(ti, tri):
    t = ti.shape[1]
    return pl.pallas_call(
        _rank_kernel,
        grid=(t // TT_RANK,),
        in_specs=[
            pl.BlockSpec((TOP_K, TT_RANK), lambda i: (0, i)),
            pl.BlockSpec((TT_RANK, TT_RANK), lambda i: (0, 0)),
        ],
        out_specs=[
            pl.BlockSpec((TOP_K, TT_RANK), lambda i: (0, i)),
            pl.BlockSpec((N_EXPERTS, 128), lambda i: (0, 0)),
        ],
        out_shape=[
            jax.ShapeDtypeStruct((TOP_K, t), I32),
            jax.ShapeDtypeStruct((N_EXPERTS, 128), I32),
        ],
        scratch_shapes=[pltpu.VMEM((N_EXPERTS, 128), F32)],
        compiler_params=_cparams(("arbitrary",)),
        name="rank",
    )(ti, tri)


def _dispatch_kernel(fill_lo_ref, fill_hi_ref, dest_ref, hp_ref, xs_ref, zero_ref, row_sem, pad_sem):
    i = pl.program_id(0)
    n_fill = fill_lo_ref.shape[0]

    def pad_copy(p):
        return pltpu.make_async_copy(zero_ref.at[pl.ds(0, 1)], xs_ref.at[pl.ds(p, 1)], pad_sem)

    def row_copy(tk, dst):
        return pltpu.make_async_copy(hp_ref.at[pl.ds(tk, 1)], xs_ref.at[pl.ds(dst, 1)], row_sem)

    @pl.when(i == 0)
    def _():
        zero_ref[...] = jnp.zeros_like(zero_ref)
        for f in range(n_fill):
            lo = fill_lo_ref[f]
            hi = fill_hi_ref[f]

            def start(p, c):
                pad_copy(p).start()
                return c

            def wait(p, c):
                pad_copy(p).wait()
                return c

            lax.fori_loop(lo, hi, start, 0)
            lax.fori_loop(lo, hi, wait, 0)

    def issue(tk, c):
        for s in range(TOP_K):
            row_copy(tk, dest_ref[s, tk]).start()
        return c

    def drain(tk, c):
        for s in range(TOP_K):
            row_copy(0, 0).wait()
        return c

    lax.fori_loop(0, TT_DISP, issue, 0, unroll=8)
    lax.fori_loop(0, TT_DISP, drain, 0, unroll=8)


def _dispatch(fill_lo, fill_hi, dest, hp, n_pad):
    t = hp.shape[0]
    grid_spec = pltpu.PrefetchScalarGridSpec(
        num_scalar_prefetch=2,
        grid=(t // TT_DISP,),
        in_specs=[
            pl.BlockSpec((TOP_K, TT_DISP), lambda i, lo, hi: (0, i), memory_space=pltpu.SMEM),
            pl.BlockSpec((TT_DISP, HALF), lambda i, lo, hi: (i, 0)),
        ],
        out_specs=pl.BlockSpec(memory_space=pl.ANY),
        scratch_shapes=[
            pltpu.VMEM((8, HALF), U32),
            pltpu.SemaphoreType.DMA(()),
            pltpu.SemaphoreType.DMA(()),
        ],
    )
    return pl.pallas_call(
        _dispatch_kernel,
        grid_spec=grid_spec,
        out_shape=jax.ShapeDtypeStruct((n_pad, HALF), U32),
        compiler_params=_cparams(("arbitrary",)),
        name="dispatch",
    )(fill_lo, fill_hi, dest, hp)


def _expert_kernel(be_ref, nu_ref, x_ref, wg_ref, wu_ref, bg_ref, bu_ref, wd_ref, bd_ref, y_ref, xb_ref, acc_ref):
    i = pl.program_id(0)
    j = pl.program_id(1)
    nf = D_FF // TF_EXP
    active = i < nu_ref[0]

    @pl.when(jnp.logical_and(active, j == 0))
    def _():
        u = x_ref[...]
        xb_ref[:, :HALF] = _unpack_lo(u).astype(BF16)
        xb_ref[:, HALF:] = _unpack_hi(u).astype(BF16)
        acc_ref[...] = jnp.zeros_like(acc_ref)

    @pl.when(active)
    def _():
        xb = xb_ref[...]
        gate = jnp.dot(xb, wg_ref[0, 0].astype(BF16), preferred_element_type=F32) + bg_ref[0, 0]
        up = jnp.dot(xb, wu_ref[0, 0].astype(BF16), preferred_element_type=F32) + bu_ref[0, 0]
        gate = jnp.minimum(gate, SWIGLU_LIMIT)
        up = jnp.clip(up, -SWIGLU_LIMIT, SWIGLU_LIMIT)
        act = (up + 1.0) * (gate * jax.nn.sigmoid(SWIGLU_ALPHA * gate))
        acc_ref[...] += jnp.dot(act.astype(BF16), wd_ref[0, 0].astype(BF16), preferred_element_type=F32)

    @pl.when(jnp.logical_and(active, j == nf - 1))
    def _():
        y_ref[...] = _pack_bf16_pairs(acc_ref[...] + bd_ref[0, 0])

    @pl.when(jnp.logical_and(jnp.logical_not(active), j == nf - 1))
    def _():
        y_ref[...] = jnp.zeros_like(y_ref)


def _experts(blk_e, n_used, xs, w_gu, b_gu, w_dn, b_dn, layer):
    n_pad = xs.shape[0]
    depth = w_gu.shape[0]
    nf = D_FF // TF_EXP

    def blk(i, nu):
        return jnp.minimum(i, nu[0] - 1)

    def ftile(i, j, nu):
        return jnp.where(i < nu[0], j, nf - 1)

    grid_spec = pltpu.PrefetchScalarGridSpec(
        num_scalar_prefetch=2,
        grid=(n_pad // TM_EXP, nf),
        in_specs=[
            pl.BlockSpec((TM_EXP, HALF), lambda i, j, be, nu: (blk(i, nu), 0)),
            pl.BlockSpec((1, 1, D_MODEL, TF_EXP), lambda i, j, be, nu: (layer, be[blk(i, nu)], 0, ftile(i, j, nu))),
            pl.BlockSpec((1, 1, D_MODEL, TF_EXP),
                         lambda i, j, be, nu: (layer, be[blk(i, nu)], 0, nf + ftile(i, j, nu))),
            pl.BlockSpec((1, 1, 1, TF_EXP), lambda i, j, be, nu: (layer, be[blk(i, nu)], 0, ftile(i, j, nu))),
            pl.BlockSpec((1, 1, 1, TF_EXP), lambda i, j, be, nu: (layer, be[blk(i, nu)], 0, nf + ftile(i, j, nu))),
            pl.BlockSpec((1, 1, TF_EXP, D_MODEL), lambda i, j, be, nu: (layer, be[blk(i, nu)], ftile(i, j, nu), 0)),
            pl.BlockSpec((1, 1, 1, D_MODEL), lambda i, j, be, nu: (layer, be[blk(i, nu)], 0, 0)),
        ],
        out_specs=pl.BlockSpec((TM_EXP, HALF), lambda i, j, be, nu: (i, 0)),
        scratch_shapes=[
            pltpu.VMEM((TM_EXP, D_MODEL), BF16),
            pltpu.VMEM((TM_EXP, D_MODEL), F32),
        ],
    )
    return pl.pallas_call(
        _expert_kernel,
        grid_spec=grid_spec,
        out_shape=jax.ShapeDtypeStruct((n_pad, HALF), U32),
        compiler_params=_cparams(("arbitrary", "arbitrary")),
        name="experts",
    )(blk_e, n_used, xs, w_gu, w_gu, b_gu.reshape(depth, N_EXPERTS, 1, 2 * D_FF),
      b_gu.reshape(depth, N_EXPERTS, 1, 2 * D_FF), w_dn, b_dn.reshape(depth, N_EXPERTS, 1, D_MODEL))


def _combine_kernel(dest_ref, gcol_ref, x_ref, g2_ref, y_ref, o_ref, buf_ref, sem):
    def row_copy(s, tk, src):
        return pltpu.make_async_copy(y_ref.at[pl.ds(src, 1)], buf_ref.at[s, pl.ds(tk, 1)], sem)

    def issue(tk, c):
        for s in range(TOP_K):
            row_copy(s, tk, dest_ref[s, tk]).start()
        return c

    def drain(tk, c):
        for s in range(TOP_K):
            row_copy(s, tk, 0).wait()
        return c

    lax.fori_loop(0, TT_COMB, issue, 0, unroll=8)
    lax.fori_loop(0, TT_COMB, drain, 0, unroll=8)

    g = gcol_ref[...]
    lo = jnp.zeros((TT_COMB, HALF), F32)
    hi = jnp.zeros((TT_COMB, HALF), F32)
    for s in range(TOP_K):
        u = buf_ref[s]
        gs = g[:, s:s + 1]
        lo = lo + _unpack_lo(u) * gs
        hi = hi + _unpack_hi(u) * gs
    g2 = g2_ref[0]
    o_ref[:, :HALF] = x_ref[:, :HALF] + g2[:, :HALF] * lo
    o_ref[:, HALF:] = x_ref[:, HALF:] + g2[:, HALF:] * hi


def _combine(dest, gcol, x1, mod_rows, y, layer, n_prompt_seg, nb_pad):
    t = x1.shape[0]

    def g2_map(i):
        b = _batch_of_tile(i, TT_COMB, n_prompt_seg)
        return ((layer * nb_pad + b) * N_MOD + 5, 0, 0)

    return pl.pallas_call(
        _combine_kernel,
        grid=(t // TT_COMB,),
        in_specs=[
            pl.BlockSpec((TOP_K, TT_COMB), lambda i: (0, i), memory_space=pltpu.SMEM),
            pl.BlockSpec((TT_COMB, 128), lambda i: (i, 0)),
            pl.BlockSpec((TT_COMB, D_MODEL), lambda i: (i, 0)),
            pl.BlockSpec((1, 1, D_MODEL), g2_map),
            pl.BlockSpec(memory_space=pl.ANY),
        ],
        out_specs=pl.BlockSpec((TT_COMB, D_MODEL), lambda i: (i, 0)),
        out_shape=jax.ShapeDtypeStruct((t, D_MODEL), F32),
        scratch_shapes=[
            pltpu.VMEM((TOP_K, TT_COMB, HALF), U32),
            pltpu.SemaphoreType.DMA(()),
        ],
        compiler_params=_cparams(("arbitrary",)),
        name="combine",
    )(dest, gcol, x1, mod_rows, y)


def _routing_plan(ti, rank, counts, n_blocks):
    pcounts = (counts + TM_EXP - 1) // TM_EXP * TM_EXP
    pend = jnp.cumsum(pcounts)
    pstart = pend - pcounts
    experts = jnp.arange(N_EXPERTS, dtype=I32)
    dest = rank + jnp.sum(jnp.where(ti[None] == experts[:, None, None], pstart[:, None, None], 0), axis=0)
    first_row = jnp.arange(n_blocks, dtype=I32) * TM_EXP
    blk_e = jnp.minimum(jnp.sum((pend[None, :] <= first_row[:, None]).astype(I32), axis=1), N_EXPERTS - 1)
    n_used = (pend[-1:] // TM_EXP).astype(I32)
    fill_lo = jnp.concatenate([pstart + counts, pend[-1:]]).astype(I32)
    fill_hi = jnp.concatenate([pend, jnp.full((1,), n_blocks * TM_EXP, I32)]).astype(I32)
    return dest.astype(I32), blk_e, n_used, fill_lo, fill_hi


def kernel(x_prompt, x_sample, c_prompt, c_sample, ada_w, ada_b, norm_mix_w, norm_ffn_w, w_in, q_norm_w, k_norm_w, rel_pos_bias, conv_w, conv_b, conv_norm_w, conv_norm_b, sgu_norm_w, sgu_norm_b, sgu_w, sgu_b, mix_norm_w, w_out, router_w, router_b, w_gate_up, b_gate_up, w_down, b_down):
    bp, n_p, d = x_prompt.shape
    bs, n_s, _ = x_sample.shape
    assert d == D_MODEL and n_p == 2 * SEG and n_s == SEG
    depth = ada_w.shape[0]
    n_prompt_seg = 2 * bp
    t = bp * n_p + bs * n_s
    nb = bp + bs
    nb_pad = -(-nb // 8) * 8

    x = jnp.concatenate([x_prompt.reshape(bp * n_p, d), x_sample.reshape(bs * n_s, d)], axis=0)
    c_pad = jnp.concatenate([c_prompt, c_sample, jnp.zeros((nb_pad - nb, d), F32)], axis=0)
    mod_rows = _adaln(c_pad, ada_w, ada_b).reshape(depth * nb_pad * N_MOD, 1, D_MODEL)

    w_in_b = w_in.astype(BF16)
    w_out_b = w_out.astype(BF16)
    rwt = jnp.swapaxes(router_w, 1, 2)
    bias_tab = _bias_table(rel_pos_bias)
    group = np.arange(TN_IN) // ATTN_HD
    ones_bd = jnp.asarray(group[:, None] == group[None, :], BF16)
    tri = jnp.asarray(np.arange(TT_RANK)[:, None] < np.arange(TT_RANK)[None, :], BF16)
    n_slot = t * TOP_K
    n_blocks = -(-(n_slot + N_EXPERTS * (TM_EXP - 1)) // TM_EXP)

    for l in range(depth):
        qn = jnp.tile(q_norm_w[l] * (ATTN_HD ** -0.5), TN_IN // ATTN_HD).reshape(1, TN_IN)
        kn = jnp.tile(k_norm_w[l], TN_IN // ATTN_HD).reshape(1, TN_IN)
        sgu_bias = jnp.repeat(sgu_b[l].T, D_SGU // SGU_GROUPS, axis=1)

        p = _inproj(x, mod_rows, norm_mix_w, w_in_b, qn, kn, ones_bd, l, n_prompt_seg, nb_pad)
        ya = _attention(p, bias_tab, l, n_prompt_seg)
        yc = _conv(p, conv_w, conv_b, conv_norm_w, conv_norm_b, l, n_prompt_seg)
        ys = _sgu(p, sgu_norm_w, sgu_norm_b, sgu_w, sgu_bias, l)
        x1, hp, ti, gcol = _outproj(ya, yc, ys, x, mod_rows, mix_norm_w, norm_ffn_w, w_out_b, rwt, router_b, l,
                                    n_prompt_seg, nb_pad)
        rank, cnt = _rank(ti, tri)
        dest, blk_e, n_used, fill_lo, fill_hi = _routing_plan(ti, rank, cnt[:, 0], n_blocks)
        xs = _dispatch(fill_lo, fill_hi, dest, hp, n_blocks * TM_EXP)
        y = _experts(blk_e, n_used, xs, w_gate_up, b_gate_up, w_down, b_down, l)
        x = _combine(dest, gcol, x1, mod_rows, y, l, n_prompt_seg, nb_pad)

    y_prompt = x[:bp * n_p].reshape(bp, n_p, d)
    y_sample = x[bp * n_p:].reshape(bs, n_s, d)
    return (y_prompt, y_sample)
```

```python
import functools

import numpy as np
import jax
import jax.numpy as jnp
from jax import lax
from jax.experimental import pallas as pl
from jax.experimental.pallas import tpu as pltpu

F32 = jnp.float32
BF16 = jnp.bfloat16
U32 = jnp.uint32
I32 = jnp.int32

D_MODEL = 2048
GRID_W = 64
ATTN_HEADS = 16
ATTN_HD = 64
D_ATTN = ATTN_HEADS * ATTN_HD
WIN_R = 8
WIN_C = 16
D_CONV = 512
CONV_K = 31
D_SGU = 512
SGU_GROUPS = 8
SGU_CHUNK = 128
D_IN = 3 * D_ATTN + 2 * D_CONV + 2 * D_SGU
N_EXPERTS = 32
TOP_K = 4
D_FF = 2048
SWIGLU_LIMIT = 7.0
SWIGLU_ALPHA = 1.702
N_MOD = 6
EPS = 1e-6

SEG = 2048
SEG_ROWS = SEG // GRID_W
HALF = D_MODEL // 2
NEG_BIAS = -1e30
HI_MASK = 0xFFFF0000
WORD_TILE = HALF // 128

V7X_VMEM_LIMIT_BYTES = 56 * 1024 * 1024

TN_ADA = 1024
TM_IN, TN_IN = 1024, 512
TM_SGU = 1024
TM_OUT, TK_OUT = 512, 512
TT_RANK = 512
TT_DISP = 512
TM_EXP, TF_EXP = 1024, 256
TT_COMB = 256
ATTN_ROW_GROUP = 8
CONV_ROWS = 32
HALO = 16


def _cparams(sem):
    return pltpu.CompilerParams(dimension_semantics=sem, vmem_limit_bytes=V7X_VMEM_LIMIT_BYTES)


def _batch_of_tile(i, tile, n_prompt_seg):
    seg = (i * tile) // SEG
    return jnp.where(seg < n_prompt_seg, seg // 2, seg - n_prompt_seg // 2)


def _pack_bf16_pairs(y):
    yb = y.astype(BF16)
    lo = pltpu.bitcast(yb[:, :HALF].astype(F32), U32) >> 16
    hi = pltpu.bitcast(yb[:, HALF:].astype(F32), U32) & jnp.uint32(HI_MASK)
    return lo | hi


def _store_token_tiles(ref, packed):
    m = packed.shape[0]
    for sl in range(WORD_TILE):
        ref[pl.ds(sl, m, stride=WORD_TILE), :] = packed[:, sl * 128:(sl + 1) * 128]


def _load_token_tile_words(ref, first_row, m, sl):
    return ref[pl.ds(first_row + sl, m, stride=WORD_TILE), :]


def _unpack_lo(u):
    return pltpu.bitcast(u << 16, F32)


def _unpack_hi(u):
    return pltpu.bitcast(u & jnp.uint32(HI_MASK), F32)


def _adaln_kernel(c_ref, w_ref, b_ref, o_ref):
    sc = jax.nn.silu(c_ref[...]).astype(BF16)
    o_ref[0] = jnp.dot(sc, w_ref[0].astype(BF16), preferred_element_type=F32) + b_ref[0]


def _adaln(c_pad, ada_w, ada_b):
    depth = ada_w.shape[0]
    nb = c_pad.shape[0]
    n_out = N_MOD * D_MODEL
    return pl.pallas_call(
        _adaln_kernel,
        grid=(depth, n_out // TN_ADA),
        in_specs=[
            pl.BlockSpec((nb, D_MODEL), lambda l, j: (0, 0)),
            pl.BlockSpec((1, D_MODEL, TN_ADA), lambda l, j: (l, 0, j)),
            pl.BlockSpec((1, 1, TN_ADA), lambda l, j: (l, 0, j)),
        ],
        out_specs=pl.BlockSpec((1, nb, TN_ADA), lambda l, j: (l, 0, j)),
        out_shape=jax.ShapeDtypeStruct((depth, nb, n_out), F32),
        compiler_params=_cparams(("arbitrary", "arbitrary")),
        name="adaln",
    )(c_pad, ada_w, ada_b.reshape(depth, 1, n_out))


def _inproj_kernel(x_ref, nw_ref, shift_ref, scale_ref, w_ref, qn_ref, kn_ref, ones_ref, o_ref, h_ref):
    j = pl.program_id(1)

    @pl.when(j == 0)
    def _():
        xf = x_ref[...]
        y = xf * lax.rsqrt(jnp.mean(xf * xf, axis=-1, keepdims=True) + EPS)
        h = (y * nw_ref[0]) * (1.0 + scale_ref[0]) + shift_ref[0]
        h_ref[...] = h.astype(BF16)

    acc = jnp.dot(h_ref[...], w_ref[0], preferred_element_type=F32)
    n_qk = 2 * D_ATTN // TN_IN

    @pl.when(j < n_qk)
    def _():
        ss = jnp.dot((acc * acc).astype(BF16), ones_ref[...], preferred_element_type=F32)
        r = lax.rsqrt(ss * (1.0 / ATTN_HD) + EPS)
        nw = jnp.where(j < n_qk // 2, qn_ref[...], kn_ref[...])
        o_ref[...] = (acc * r * nw).astype(o_ref.dtype)

    @pl.when(j >= n_qk)
    def _():
        o_ref[...] = acc.astype(o_ref.dtype)


def _inproj(x, mod_rows, norm_w, w_bf16, qn, kn, ones_bd, layer, n_prompt_seg, nb_pad):
    t = x.shape[0]
    depth = norm_w.shape[0]

    def mod_map(k):
        def f(i, j):
            b = _batch_of_tile(i, TM_IN, n_prompt_seg)
            return ((layer * nb_pad + b) * N_MOD + k, 0, 0)
        return f

    return pl.pallas_call(
        _inproj_kernel,
        grid=(t // TM_IN, D_IN // TN_IN),
        in_specs=[
            pl.BlockSpec((TM_IN, D_MODEL), lambda i, j: (i, 0)),
            pl.BlockSpec((1, 1, D_MODEL), lambda i, j: (layer, 0, 0)),
            pl.BlockSpec((1, 1, D_MODEL), mod_map(0)),
            pl.BlockSpec((1, 1, D_MODEL), mod_map(1)),
            pl.BlockSpec((1, D_MODEL, TN_IN), lambda i, j: (layer, 0, j)),
            pl.BlockSpec((1, TN_IN), lambda i, j: (0, 0)),
            pl.BlockSpec((1, TN_IN), lambda i, j: (0, 0)),
            pl.BlockSpec((TN_IN, TN_IN), lambda i, j: (0, 0)),
        ],
        out_specs=pl.BlockSpec((TM_IN, TN_IN), lambda i, j: (i, j)),
        out_shape=jax.ShapeDtypeStruct((t, D_IN), BF16),
        scratch_shapes=[pltpu.VMEM((TM_IN, D_MODEL), BF16)],
        compiler_params=_cparams(("arbitrary", "arbitrary")),
        name="inproj",
    )(x, norm_w.reshape(depth, 1, D_MODEL), mod_rows, mod_rows, w_bf16, qn, kn, ones_bd)


def _attn_kernel(q_ref, k_ref, v_ref, b_ref, o_ref, *, n_prompt_seg):
    seg = pl.program_id(0)
    is_prompt = seg < n_prompt_seg
    half = seg % 2
    rows = jnp.where(is_prompt, 2 * SEG_ROWS, SEG_ROWS)
    row0 = jnp.where(is_prompt, half * SEG_ROWS, 0)
    kv0 = jnp.where(is_prompt, 0, half * SEG)
    lane = lax.broadcasted_iota(I32, (GRID_W, 2 * ATTN_HD), 1)
    first_head = lane < ATTN_HD
    band = WIN_R * GRID_W

    def group_body(gi, carry):
        q0s, k0s, scores = [], [], []
        for g in range(ATTN_ROW_GROUP):
            rr = gi * ATTN_ROW_GROUP + g
            r = row0 + rr
            rs = jnp.clip(r - WIN_R // 2, 0, rows - WIN_R)
            di = r - rs
            q0 = pl.multiple_of(rr * GRID_W, GRID_W)
            k0 = pl.multiple_of(kv0 + rs * GRID_W, GRID_W)
            q = q_ref[pl.ds(q0, GRID_W), :]
            zero = jnp.zeros_like(q)
            qm = jnp.concatenate([jnp.where(first_head, q, zero), jnp.where(first_head, zero, q)], axis=0)
            s = lax.dot_general(qm, k_ref[pl.ds(k0, band), :], (((1,), (1,)), ((), ())),
                                preferred_element_type=F32)
            scores.append(s + b_ref[0, 0, di])
            q0s.append(q0)
            k0s.append(k0)
        probs, inv = [], []
        for s in scores:
            m = jnp.max(s, axis=-1, keepdims=True)
            p = jnp.exp(s - m)
            inv.append(1.0 / jnp.sum(p, axis=-1, keepdims=True))
            probs.append(p.astype(BF16))
        for g in range(ATTN_ROW_GROUP):
            o = jnp.dot(probs[g], v_ref[pl.ds(k0s[g], band), :], preferred_element_type=F32) * inv[g]
            o_ref[pl.ds(q0s[g], GRID_W), :] = jnp.where(first_head, o[:GRID_W], o[GRID_W:]).astype(o_ref.dtype)
        return carry

    lax.fori_loop(0, SEG_ROWS // ATTN_ROW_GROUP, group_body, 0)


def _attention(p, bias_tab, layer, n_prompt_seg):
    t = p.shape[0]
    n_seg = t // SEG
    n_pair = ATTN_HEADS // 2
    lanes = 2 * ATTN_HD
    return pl.pallas_call(
        functools.partial(_attn_kernel, n_prompt_seg=n_prompt_seg),
        grid=(n_seg, n_pair),
        in_specs=[
            pl.BlockSpec((SEG, lanes), lambda s, h: (s, h)),
            pl.BlockSpec((2 * SEG, lanes), lambda s, h: (s // 2, n_pair + h)),
            pl.BlockSpec((2 * SEG, lanes), lambda s, h: (s // 2, 2 * n_pair + h)),
            pl.BlockSpec((1, 1, WIN_R, 2 * GRID_W, WIN_R * GRID_W), lambda s, h: (layer, h, 0, 0, 0)),
        ],
        out_specs=pl.BlockSpec((SEG, lanes), lambda s, h: (s, h)),
        out_shape=jax.ShapeDtypeStruct((t, D_ATTN), BF16),
        compiler_params=_cparams(("arbitrary", "arbitrary")),
        name="attention",
    )(p, p, p, bias_tab)


def _bias_table(rel_pos_bias):
    c = np.arange(GRID_W)
    cs = np.clip(c - WIN_C // 2, 0, GRID_W - WIN_C)
    kc = np.arange(GRID_W)
    valid = (kc[None, :] >= cs[:, None]) & (kc[None, :] < cs[:, None] + WIN_C)
    col_rel = np.clip(kc[None, :] - c[:, None] + (WIN_C - 1), 0, 2 * WIN_C - 2)
    depth = rel_pos_bias.shape[0]
    n_col = 2 * WIN_C - 1
    onehot = jnp.asarray(np.arange(n_col)[:, None] == col_rel.reshape(1, -1), F32)
    cols = jnp.einsum('lhrx,xk->lhrk', rel_pos_bias.astype(F32), onehot, precision=lax.Precision.HIGHEST)
    cols = jnp.where(valid.reshape(1, 1, 1, -1), cols, NEG_BIAS)
    cols = cols.reshape(depth, ATTN_HEADS, 2 * WIN_R - 1, GRID_W, GRID_W)
    tab = jnp.stack([cols[:, :, WIN_R - 1 - di:2 * WIN_R - 1 - di] for di in range(WIN_R)], axis=2)
    tab = jnp.swapaxes(tab, 3, 4).reshape(depth, ATTN_HEADS // 2, 2, WIN_R, GRID_W, WIN_R * GRID_W)
    return jnp.swapaxes(tab, 2, 3).reshape(depth, ATTN_HEADS // 2, WIN_R, 2 * GRID_W, WIN_R * GRID_W)


def _conv_kernel(am_ref, gm_ref, ap_ref, gp_ref, an_ref, gn_ref, w_ref, cb_ref, lw_ref, lb_ref, o_ref, buf_ref,
                 *, n_prompt_seg):
    seg = pl.program_id(0)
    is_prompt = seg < n_prompt_seg
    half = seg % 2
    has_prev = jnp.logical_and(is_prompt, half == 1)
    has_next = jnp.logical_and(is_prompt, half == 0)

    def glu(a_ref, g_ref):
        return a_ref[...].astype(F32) * jax.nn.sigmoid(g_ref[...].astype(F32))

    buf_ref[pl.ds(HALO, SEG), :] = glu(am_ref, gm_ref)
    buf_ref[pl.ds(0, HALO), :] = jnp.where(has_prev, glu(ap_ref, gp_ref), 0.0)
    buf_ref[pl.ds(HALO + SEG, HALO), :] = jnp.where(has_next, glu(an_ref, gn_ref), 0.0)

    w = w_ref[0]
    off = HALO - CONV_K // 2

    def chunk(ci, carry):
        r0 = pl.multiple_of(ci * CONV_ROWS, CONV_ROWS)
        win_rows = CONV_ROWS + 32
        win = buf_ref[pl.ds(r0, win_rows), :]
        acc = jnp.zeros((CONV_ROWS, D_CONV), F32)
        for s in range(8):
            ws = win if s == 0 else pltpu.roll(win, win_rows - s, axis=0)
            for a in range(4):
                k = 8 * a + s - off
                if 0 <= k < CONV_K:
                    acc = acc + ws[8 * a:8 * a + CONV_ROWS, :] * w[k:k + 1, :]
        y = acc + cb_ref[0]
        mu = jnp.mean(y, axis=-1, keepdims=True)
        yc = y - mu
        var = jnp.mean(yc * yc, axis=-1, keepdims=True)
        yn = yc * lax.rsqrt(var + EPS) * lw_ref[0] + lb_ref[0]
        o_ref[pl.ds(r0, CONV_ROWS), :] = jax.nn.silu(yn).astype(o_ref.dtype)
        return carry

    lax.fori_loop(0, SEG // CONV_ROWS, chunk, 0)


def _conv(p, conv_w, conv_b, norm_w, norm_b, layer, n_prompt_seg):
    t = p.shape[0]
    depth = conv_w.shape[0]
    n_seg = t // SEG
    ca = 3 * D_ATTN // D_CONV
    cg = ca + 1
    per_seg = SEG // HALO
    last = t // HALO - 1

    def prev_map(col):
        return lambda s: (jnp.maximum(s * per_seg - 1, 0), col)

    def next_map(col):
        return lambda s: (jnp.minimum((s + 1) * per_seg, last), col)

    vec = lambda: pl.BlockSpec((1, 1, D_CONV), lambda s: (layer, 0, 0))
    return pl.pallas_call(
        functools.partial(_conv_kernel, n_prompt_seg=n_prompt_seg),
        grid=(n_seg,),
        in_specs=[
            pl.BlockSpec((SEG, D_CONV), lambda s: (s, ca)),
            pl.BlockSpec((SEG, D_CONV), lambda s: (s, cg)),
            pl.BlockSpec((HALO, D_CONV), prev_map(ca)),
            pl.BlockSpec((HALO, D_CONV), prev_map(cg)),
            pl.BlockSpec((HALO, D_CONV), next_map(ca)),
            pl.BlockSpec((HALO, D_CONV), next_map(cg)),
            pl.BlockSpec((1, CONV_K, D_CONV), lambda s: (layer, 0, 0)),
            vec(), vec(), vec(),
        ],
        out_specs=pl.BlockSpec((SEG, D_CONV), lambda s: (s, 0)),
        out_shape=jax.ShapeDtypeStruct((t, D_CONV), BF16),
        scratch_shapes=[pltpu.VMEM((SEG + 2 * HALO, D_CONV), F32)],
        compiler_params=_cparams(("arbitrary",)),
        name="conv",
    )(p, p, p, p, p, p, conv_w, conv_b.reshape(depth, 1, D_CONV), norm_w.reshape(depth, 1, D_CONV),
      norm_b.reshape(depth, 1, D_CONV))


def _sgu_kernel(u_ref, v_ref, lw_ref, lb_ref, ws_ref, bs_ref, o_ref):
    u = jax.nn.gelu(u_ref[...].astype(F32))
    g = jax.nn.gelu(v_ref[...].astype(F32))
    mu = jnp.mean(g, axis=-1, keepdims=True)
    gc = g - mu
    var = jnp.mean(gc * gc, axis=-1, keepdims=True)
    v = (gc * lax.rsqrt(var + EPS) * lw_ref[0] + lb_ref[0]).astype(BF16)
    lanes = 2 * (D_SGU // SGU_GROUPS)
    first_group = lax.broadcasted_iota(I32, (SGU_CHUNK, lanes), 1) < lanes // 2
    for c in range(TM_SGU // SGU_CHUNK):
        rows = slice(c * SGU_CHUNK, (c + 1) * SGU_CHUNK)
        for pair in range(SGU_GROUPS // 2):
            cols = slice(pair * lanes, (pair + 1) * lanes)
            vp = v[rows, cols]
            m0 = jnp.dot(ws_ref[0, 2 * pair].astype(BF16), vp, preferred_element_type=F32)
            m1 = jnp.dot(ws_ref[0, 2 * pair + 1].astype(BF16), vp, preferred_element_type=F32)
            mixed = jnp.where(first_group, m0, m1) + bs_ref[:, cols]
            o_ref[rows, cols] = (u[rows, cols] * mixed).astype(o_ref.dtype)


def _sgu(p, norm_w, norm_b, sgu_w, bias_exp, layer):
    t = p.shape[0]
    depth = norm_w.shape[0]
    su = (3 * D_ATTN + 2 * D_CONV) // D_SGU
    sv = su + 1
    vec = lambda: pl.BlockSpec((1, 1, D_SGU), lambda i: (layer, 0, 0))
    return pl.pallas_call(
        _sgu_kernel,
        grid=(t // TM_SGU,),
        in_specs=[
            pl.BlockSpec((TM_SGU, D_SGU), lambda i: (i, su)),
            pl.BlockSpec((TM_SGU, D_SGU), lambda i: (i, sv)),
            vec(), vec(),
            pl.BlockSpec((1, SGU_GROUPS, SGU_CHUNK, SGU_CHUNK), lambda i: (layer, 0, 0, 0)),
            pl.BlockSpec((SGU_CHUNK, D_SGU), lambda i: (0, 0)),
        ],
        out_specs=pl.BlockSpec((TM_SGU, D_SGU), lambda i: (i, 0)),
        out_shape=jax.ShapeDtypeStruct((t, D_SGU), BF16),
        compiler_params=_cparams(("arbitrary",)),
        name="sgu",
    )(p, p, norm_w.reshape(depth, 1, D_SGU), norm_b.reshape(depth, 1, D_SGU), sgu_w, bias_exp)


def _rms(y, w):
    return y * lax.rsqrt(jnp.mean(y * y, axis=-1, keepdims=True) + EPS) * w


def _outproj_kernel(ya_ref, yc_ref, ys_ref, x_ref, mw_ref, g1_ref, nw_ref, sh2_ref, sc2_ref, w_ref, rwt_ref, rb_ref,
                    x1_ref, hp_ref, ti_ref, gcol_ref, mix_ref, acc_ref):
    k = pl.program_id(1)
    nk = D_MODEL // TK_OUT

    @pl.when(k == 0)
    def _():
        mw = mw_ref[0]
        na = _rms(ya_ref[...].astype(F32), mw[:, :D_ATTN])
        nc = _rms(yc_ref[...].astype(F32), mw[:, D_ATTN:D_ATTN + D_CONV])
        ns = _rms(ys_ref[...].astype(F32), mw[:, D_ATTN + D_CONV:])
        mix_ref[0] = na[:, :TK_OUT].astype(BF16)
        mix_ref[1] = na[:, TK_OUT:].astype(BF16)
        mix_ref[2] = nc.astype(BF16)
        mix_ref[3] = ns.astype(BF16)
        acc_ref[...] = jnp.zeros_like(acc_ref)

    acc_ref[...] += jnp.dot(mix_ref[k], w_ref[0], preferred_element_type=F32)

    @pl.when(k == nk - 1)
    def _():
        x1 = x_ref[...] + g1_ref[0] * acc_ref[...]
        x1_ref[...] = x1
        h2 = _rms(x1, nw_ref[0]) * (1.0 + sc2_ref[0]) + sh2_ref[0]
        _store_token_tiles(hp_ref, _pack_bf16_pairs(h2))

        hi = h2.astype(BF16)
        lo = (h2 - hi.astype(F32)).astype(BF16)
        rw = rwt_ref[0]
        rwh = rw.astype(BF16)
        rwl = (rw - rwh.astype(F32)).astype(BF16)
        dn = (((1,), (1,)), ((), ()))
        logits = (lax.dot_general(rwh, hi, dn, preferred_element_type=F32)
                  + lax.dot_general(rwh, lo, dn, preferred_element_type=F32)
                  + lax.dot_general(rwl, hi, dn, preferred_element_type=F32)) + rb_ref[0]

        tm = logits.shape[1]
        ie = lax.broadcasted_iota(I32, (N_EXPERTS, tm), 0)
        work = logits
        vals, idxs = [], []
        for _ in range(TOP_K):
            m = jnp.max(work, axis=0, keepdims=True)
            idx = jnp.min(jnp.where(work == m, ie, N_EXPERTS), axis=0, keepdims=True)
            vals.append(m)
            idxs.append(idx)
            work = jnp.where(ie == idx, -jnp.inf, work)
        ex = [jnp.exp(v - vals[0]) for v in vals]
        den = ex[0] + ex[1] + ex[2] + ex[3]
        ti_ref[...] = jnp.concatenate(idxs, axis=0)
        gates = jnp.concatenate([e / den for e in ex] + [jnp.zeros((128 - TOP_K, tm), F32)], axis=0)
        gcol_ref[...] = gates.T


def _outproj(ya, yc, ys, x, mod_rows, mix_w, ffn_w, w_bf16, rwt, rb, layer, n_prompt_seg, nb_pad):
    t = x.shape[0]
    depth = mix_w.shape[0]

    def mod_map(k):
        def f(i, kk):
            b = _batch_of_tile(i, TM_OUT, n_prompt_seg)
            return ((layer * nb_pad + b) * N_MOD + k, 0, 0)
        return f

    lvec = lambda: pl.BlockSpec((1, 1, D_MODEL), lambda i, k: (layer, 0, 0))
    return pl.pallas_call(
        _outproj_kernel,
        grid=(t // TM_OUT, D_MODEL // TK_OUT),
        in_specs=[
            pl.BlockSpec((TM_OUT, D_ATTN), lambda i, k: (i, 0)),
            pl.BlockSpec((TM_OUT, D_CONV), lambda i, k: (i, 0)),
            pl.BlockSpec((TM_OUT, D_SGU), lambda i, k: (i, 0)),
            pl.BlockSpec((TM_OUT, D_MODEL), lambda i, k: (i, 0)),
            lvec(),
            pl.BlockSpec((1, 1, D_MODEL), mod_map(2)),
            lvec(),
            pl.BlockSpec((1, 1, D_MODEL), mod_map(3)),
            pl.BlockSpec((1, 1, D_MODEL), mod_map(4)),
            pl.BlockSpec((1, TK_OUT, D_MODEL), lambda i, k: (layer, k, 0)),
            pl.BlockSpec((1, N_EXPERTS, D_MODEL), lambda i, k: (layer, 0, 0)),
            pl.BlockSpec((1, N_EXPERTS, 1), lambda i, k: (layer, 0, 0)),
        ],
        out_specs=[
            pl.BlockSpec((TM_OUT, D_MODEL), lambda i, k: (i, 0)),
            pl.BlockSpec((TM_OUT * WORD_TILE, 128), lambda i, k: (i, 0)),
            pl.BlockSpec((TOP_K, TM_OUT), lambda i, k: (0, i)),
            pl.BlockSpec((TM_OUT, 128), lambda i, k: (i, 0)),
        ],
        out_shape=[
            jax.ShapeDtypeStruct((t, D_MODEL), F32),
            jax.ShapeDtypeStruct((t * WORD_TILE, 128), U32),
            jax.ShapeDtypeStruct((TOP_K, t), I32),
            jax.ShapeDtypeStruct((t, 128), F32),
        ],
        scratch_shapes=[
            pltpu.VMEM((D_MODEL // TK_OUT, TM_OUT, TK_OUT), BF16),
            pltpu.VMEM((TM_OUT, D_MODEL), F32),
        ],
        compiler_params=_cparams(("arbitrary", "arbitrary")),
        name="outproj",
    )(ya, yc, ys, x, mix_w.reshape(depth, 1, D_MODEL), mod_rows, ffn_w.reshape(depth, 1, D_MODEL), mod_rows,
      mod_rows, w_bf16, rwt, rb.reshape(depth, N_EXPERTS, 1))


def _rank_kernel(ti_ref, tri_ref, rank_ref, cnt_ref, base_ref):
    i = pl.program_id(0)

    @pl.when(i == 0)
    def _():
        base_ref[...] = jnp.zeros_like(base_ref)

    e = ti_ref[...]
    tt = e.shape[1]
    ie = lax.broadcasted_iota(I32, (N_EXPERTS, tt), 0)
    pre = base_ref[:, 0:1]
    ranks = []
    for s in range(TOP_K):
        hit = ie == e[s:s + 1, :]
        hf = hit.astype(F32)
        earlier = jnp.dot(hf.astype(BF16), tri_ref[...], preferred_element_type=F32)
        ranks.append(jnp.sum(jnp.where(hit, pre + earlier, 0.0), axis=0, keepdims=True))
        pre = pre + jnp.sum(hf, axis=1, keepdims=True)
    rank_ref[...] = jnp.concatenate(ranks, axis=0).astype(I32)
    total = jnp.broadcast_to(pre, base_ref.shape)
    base_ref[...] = total
    cnt_ref[...] = total.astype(I32)


def _rank(ti, tri):
    t = ti.shape[1]
    return pl.pallas_call(
        _rank_kernel,
        grid=(t // TT_RANK,),
        in_specs=[
            pl.BlockSpec((TOP_K, TT_RANK), lambda i: (0, i)),
            pl.BlockSpec((TT_RANK, TT_RANK), lambda i: (0, 0)),
        ],
        out_specs=[
            pl.BlockSpec((TOP_K, TT_RANK), lambda i: (0, i)),
            pl.BlockSpec((N_EXPERTS, 128), lambda i: (0, 0)),
        ],
        out_shape=[
            jax.ShapeDtypeStruct((TOP_K, t), I32),
            jax.ShapeDtypeStruct((N_EXPERTS, 128), I32),
        ],
        scratch_shapes=[pltpu.VMEM((N_EXPERTS, 128), F32)],
        compiler_params=_cparams(("arbitrary",)),
        name="rank",
    )(ti, tri)


def _dispatch_kernel(fill_lo_ref, fill_hi_ref, dest_ref, hp_ref, xs_ref, zero_ref, row_sem, pad_sem):
    i = pl.program_id(0)
    n_fill = fill_lo_ref.shape[0]

    def tile(row):
        return pl.ds(pl.multiple_of(row * WORD_TILE, WORD_TILE), WORD_TILE)

    def pad_copy(p):
        return pltpu.make_async_copy(zero_ref, xs_ref.at[tile(p)], pad_sem)

    def row_copy(tk, dst):
        return pltpu.make_async_copy(hp_ref.at[tile(tk)], xs_ref.at[tile(dst)], row_sem)

    @pl.when(i == 0)
    def _():
        zero_ref[...] = jnp.zeros_like(zero_ref)
        for f in range(n_fill):
            lo = fill_lo_ref[f]
            hi = fill_hi_ref[f]

            def start(p, c):
                pad_copy(p).start()
                return c

            def wait(p, c):
                pad_copy(p).wait()
                return c

            lax.fori_loop(lo, hi, start, 0)
            lax.fori_loop(lo, hi, wait, 0)

    def issue(tk, c):
        for s in range(TOP_K):
            row_copy(tk, dest_ref[s, tk]).start()
        return c

    def drain(tk, c):
        for s in range(TOP_K):
            row_copy(0, 0).wait()
        return c

    lax.fori_loop(0, TT_DISP, issue, 0, unroll=8)
    lax.fori_loop(0, TT_DISP, drain, 0, unroll=8)


def _dispatch(fill_lo, fill_hi, dest, hp, n_pad):
    t = hp.shape[0] // WORD_TILE
    grid_spec = pltpu.PrefetchScalarGridSpec(
        num_scalar_prefetch=2,
        grid=(t // TT_DISP,),
        in_specs=[
            pl.BlockSpec((TOP_K, TT_DISP), lambda i, lo, hi: (0, i), memory_space=pltpu.SMEM),
            pl.BlockSpec((TT_DISP * WORD_TILE, 128), lambda i, lo, hi: (i, 0)),
        ],
        out_specs=pl.BlockSpec(memory_space=pl.ANY),
        scratch_shapes=[
            pltpu.VMEM((WORD_TILE, 128), U32),
            pltpu.SemaphoreType.DMA(()),
            pltpu.SemaphoreType.DMA(()),
        ],
    )
    return pl.pallas_call(
        _dispatch_kernel,
        grid_spec=grid_spec,
        out_shape=jax.ShapeDtypeStruct((n_pad * WORD_TILE, 128), U32),
        compiler_params=_cparams(("arbitrary",)),
        name="dispatch",
    )(fill_lo, fill_hi, dest, hp)


def _expert_kernel(be_ref, nu_ref, x_ref, wg_ref, wu_ref, bg_ref, bu_ref, wd_ref, bd_ref, y_ref, xb_ref, acc_ref):
    i = pl.program_id(0)
    j = pl.program_id(1)
    nf = D_FF // TF_EXP
    active = i < nu_ref[0]

    @pl.when(jnp.logical_and(active, j == 0))
    def _():
        for sl in range(WORD_TILE):
            u = _load_token_tile_words(x_ref, 0, TM_EXP, sl)
            xb_ref[:, sl * 128:(sl + 1) * 128] = _unpack_lo(u).astype(BF16)
            xb_ref[:, HALF + sl * 128:HALF + (sl + 1) * 128] = _unpack_hi(u).astype(BF16)
        acc_ref[...] = jnp.zeros_like(acc_ref)

    @pl.when(active)
    def _():
        xb = xb_ref[...]
        gate = jnp.dot(xb, wg_ref[0, 0].astype(BF16), preferred_element_type=F32) + bg_ref[0, 0]
        up = jnp.dot(xb, wu_ref[0, 0].astype(BF16), preferred_element_type=F32) + bu_ref[0, 0]
        gate = jnp.minimum(gate, SWIGLU_LIMIT)
        up = jnp.clip(up, -SWIGLU_LIMIT, SWIGLU_LIMIT)
        act = (up + 1.0) * (gate * jax.nn.sigmoid(SWIGLU_ALPHA * gate))
        acc_ref[...] += jnp.dot(act.astype(BF16), wd_ref[0, 0].astype(BF16), preferred_element_type=F32)

    @pl.when(jnp.logical_and(active, j == nf - 1))
    def _():
        _store_token_tiles(y_ref, _pack_bf16_pairs(acc_ref[...] + bd_ref[0, 0]))

    @pl.when(jnp.logical_and(jnp.logical_not(active), j == nf - 1))
    def _():
        y_ref[...] = jnp.zeros_like(y_ref)


def _experts(blk_e, n_used, xs, w_gu, b_gu, w_dn, b_dn, layer):
    n_pad = xs.shape[0] // WORD_TILE
    depth = w_gu.shape[0]
    nf = D_FF // TF_EXP

    def blk(i, nu):
        return jnp.minimum(i, nu[0] - 1)

    def ftile(i, j, nu):
        return jnp.where(i < nu[0], j, nf - 1)

    grid_spec = pltpu.PrefetchScalarGridSpec(
        num_scalar_prefetch=2,
        grid=(n_pad // TM_EXP, nf),
        in_specs=[
            pl.BlockSpec((TM_EXP * WORD_TILE, 128), lambda i, j, be, nu: (blk(i, nu), 0)),
            pl.BlockSpec((1, 1, D_MODEL, TF_EXP), lambda i, j, be, nu: (layer, be[blk(i, nu)], 0, ftile(i, j, nu))),
            pl.BlockSpec((1, 1, D_MODEL, TF_EXP),
                         lambda i, j, be, nu: (layer, be[blk(i, nu)], 0, nf + ftile(i, j, nu))),
            pl.BlockSpec((1, 1, 1, TF_EXP), lambda i, j, be, nu: (layer, be[blk(i, nu)], 0, ftile(i, j, nu))),
            pl.BlockSpec((1, 1, 1, TF_EXP), lambda i, j, be, nu: (layer, be[blk(i, nu)], 0, nf + ftile(i, j, nu))),
            pl.BlockSpec((1, 1, TF_EXP, D_MODEL), lambda i, j, be, nu: (layer, be[blk(i, nu)], ftile(i, j, nu), 0)),
            pl.BlockSpec((1, 1, 1, D_MODEL), lambda i, j, be, nu: (layer, be[blk(i, nu)], 0, 0)),
        ],
        out_specs=pl.BlockSpec((TM_EXP * WORD_TILE, 128), lambda i, j, be, nu: (i, 0)),
        scratch_shapes=[
            pltpu.VMEM((TM_EXP, D_MODEL), BF16),
            pltpu.VMEM((TM_EXP, D_MODEL), F32),
        ],
    )
    return pl.pallas_call(
        _expert_kernel,
        grid_spec=grid_spec,
        out_shape=jax.ShapeDtypeStruct((n_pad * WORD_TILE, 128), U32),
        compiler_params=_cparams(("arbitrary", "arbitrary")),
        name="experts",
    )(blk_e, n_used, xs, w_gu, w_gu, b_gu.reshape(depth, N_EXPERTS, 1, 2 * D_FF),
      b_gu.reshape(depth, N_EXPERTS, 1, 2 * D_FF), w_dn, b_dn.reshape(depth, N_EXPERTS, 1, D_MODEL))


def _combine_kernel(dest_ref, gcol_ref, x_ref, g2_ref, y_ref, o_ref, buf_ref, sem):
    def tile(row):
        return pl.ds(pl.multiple_of(row * WORD_TILE, WORD_TILE), WORD_TILE)

    def row_copy(s, tk, src):
        return pltpu.make_async_copy(y_ref.at[tile(src)], buf_ref.at[tile(s * TT_COMB + tk)], sem)

    def issue(tk, c):
        for s in range(TOP_K):
            row_copy(s, tk, dest_ref[s, tk]).start()
        return c

    def drain(tk, c):
        for s in range(TOP_K):
            row_copy(s, tk, 0).wait()
        return c

    lax.fori_loop(0, TT_COMB, issue, 0, unroll=8)
    lax.fori_loop(0, TT_COMB, drain, 0, unroll=8)

    g = gcol_ref[...]
    g2 = g2_ref[0]
    gate = [jnp.broadcast_to(g[:, s:s + 1], (TT_COMB, 128)) for s in range(TOP_K)]
    for sl in range(WORD_TILE):
        lo = jnp.zeros((TT_COMB, 128), F32)
        hi = jnp.zeros((TT_COMB, 128), F32)
        for s in range(TOP_K):
            u = _load_token_tile_words(buf_ref, s * TT_COMB * WORD_TILE, TT_COMB, sl)
            lo = lo + _unpack_lo(u) * gate[s]
            hi = hi + _unpack_hi(u) * gate[s]
        c_lo = slice(sl * 128, (sl + 1) * 128)
        c_hi = slice(HALF + sl * 128, HALF + (sl + 1) * 128)
        o_ref[:, c_lo] = x_ref[:, c_lo] + g2[:, c_lo] * lo
        o_ref[:, c_hi] = x_ref[:, c_hi] + g2[:, c_hi] * hi


def _combine(dest, gcol, x1, mod_rows, y, layer, n_prompt_seg, nb_pad):
    t = x1.shape[0]

    def g2_map(i):
        b = _batch_of_tile(i, TT_COMB, n_prompt_seg)
        return ((layer * nb_pad + b) * N_MOD + 5, 0, 0)

    return pl.pallas_call(
        _combine_kernel,
        grid=(t // TT_COMB,),
        in_specs=[
            pl.BlockSpec((TOP_K, TT_COMB), lambda i: (0, i), memory_space=pltpu.SMEM),
            pl.BlockSpec((TT_COMB, 128), lambda i: (i, 0)),
            pl.BlockSpec((TT_COMB, D_MODEL), lambda i: (i, 0)),
            pl.BlockSpec((1, 1, D_MODEL), g2_map),
            pl.BlockSpec(memory_space=pl.ANY),
        ],
        out_specs=pl.BlockSpec((TT_COMB, D_MODEL), lambda i: (i, 0)),
        out_shape=jax.ShapeDtypeStruct((t, D_MODEL), F32),
        scratch_shapes=[
            pltpu.VMEM((TOP_K * TT_COMB * WORD_TILE, 128), U32),
            pltpu.SemaphoreType.DMA(()),
        ],
        compiler_params=_cparams(("arbitrary",)),
        name="combine",
    )(dest, gcol, x1, mod_rows, y)


def _routing_plan(ti, rank, counts, n_blocks):
    pcounts = (counts + TM_EXP - 1) // TM_EXP * TM_EXP
    pend = jnp.cumsum(pcounts)
    pstart = pend - pcounts
    experts = jnp.arange(N_EXPERTS, dtype=I32)
    dest = rank + jnp.sum(jnp.where(ti[None] == experts[:, None, None], pstart[:, None, None], 0), axis=0)
    first_row = jnp.arange(n_blocks, dtype=I32) * TM_EXP
    blk_e = jnp.minimum(jnp.sum((pend[None, :] <= first_row[:, None]).astype(I32), axis=1), N_EXPERTS - 1)
    n_used = (pend[-1:] // TM_EXP).astype(I32)
    fill_lo = jnp.concatenate([pstart + counts, pend[-1:]]).astype(I32)
    fill_hi = jnp.concatenate([pend, jnp.full((1,), n_blocks * TM_EXP, I32)]).astype(I32)
    return dest.astype(I32), blk_e, n_used, fill_lo, fill_hi


def kernel(x_prompt, x_sample, c_prompt, c_sample, ada_w, ada_b, norm_mix_w, norm_ffn_w, w_in, q_norm_w, k_norm_w, rel_pos_bias, conv_w, conv_b, conv_norm_w, conv_norm_b, sgu_norm_w, sgu_norm_b, sgu_w, sgu_b, mix_norm_w, w_out, router_w, router_b, w_gate_up, b_gate_up, w_down, b_down):
    bp, n_p, d = x_prompt.shape
    bs, n_s, _ = x_sample.shape
    assert d == D_MODEL and n_p == 2 * SEG and n_s == SEG
    depth = ada_w.shape[0]
    n_prompt_seg = 2 * bp
    t = bp * n_p + bs * n_s
    nb = bp + bs
    nb_pad = -(-nb // 8) * 8

    x = jnp.concatenate([x_prompt.reshape(bp * n_p, d), x_sample.reshape(bs * n_s, d)], axis=0)
    c_pad = jnp.concatenate([c_prompt, c_sample, jnp.zeros((nb_pad - nb, d), F32)], axis=0)
    mod_rows = _adaln(c_pad, ada_w, ada_b).reshape(depth * nb_pad * N_MOD, 1, D_MODEL)

    w_in_b = w_in.astype(BF16)
    w_out_b = w_out.astype(BF16)
    rwt = jnp.swapaxes(router_w, 1, 2)
    bias_tab = _bias_table(rel_pos_bias)
    group = np.arange(TN_IN) // ATTN_HD
    ones_bd = jnp.asarray(group[:, None] == group[None, :], BF16)
    tri = jnp.asarray(np.arange(TT_RANK)[:, None] < np.arange(TT_RANK)[None, :], BF16)
    n_slot = t * TOP_K
    n_blocks = -(-(n_slot + N_EXPERTS * (TM_EXP - 1)) // TM_EXP)

    for l in range(depth):
        qn = jnp.tile(q_norm_w[l] * (ATTN_HD ** -0.5), TN_IN // ATTN_HD).reshape(1, TN_IN)
        kn = jnp.tile(k_norm_w[l], TN_IN // ATTN_HD).reshape(1, TN_IN)
        sgu_bias = jnp.repeat(sgu_b[l].T, D_SGU // SGU_GROUPS, axis=1)

        p = _inproj(x, mod_rows, norm_mix_w, w_in_b, qn, kn, ones_bd, l, n_prompt_seg, nb_pad)
        ya = _attention(p, bias_tab, l, n_prompt_seg)
        yc = _conv(p, conv_w, conv_b, conv_norm_w, conv_norm_b, l, n_prompt_seg)
        ys = _sgu(p, sgu_norm_w, sgu_norm_b, sgu_w, sgu_bias, l)
        x1, hp, ti, gcol = _outproj(ya, yc, ys, x, mod_rows, mix_norm_w, norm_ffn_w, w_out_b, rwt, router_b, l,
                                    n_prompt_seg, nb_pad)
        rank, cnt = _rank(ti, tri)
        dest, blk_e, n_used, fill_lo, fill_hi = _routing_plan(ti, rank, cnt[:, 0], n_blocks)
        xs = _dispatch(fill_lo, fill_hi, dest, hp, n_blocks * TM_EXP)
        y = _experts(blk_e, n_used, xs, w_gate_up, b_gate_up, w_down, b_down, l)
        x = _combine(dest, gcol, x1, mod_rows, y, l, n_prompt_seg, nb_pad)

    y_prompt = x[:bp * n_p].reshape(bp, n_p, d)
    y_sample = x[bp * n_p:].reshape(bs, n_s, d)
    return (y_prompt, y_sample)
```

```python
import functools

import numpy as np
import jax
import jax.numpy as jnp
from jax import lax
from jax.experimental import pallas as pl
from jax.experimental.pallas import tpu as pltpu

F32 = jnp.float32
BF16 = jnp.bfloat16
U32 = jnp.uint32
I32 = jnp.int32

D_MODEL = 2048
GRID_W = 64
ATTN_HEADS = 16
ATTN_HD = 64
D_ATTN = ATTN_HEADS * ATTN_HD
WIN_R = 8
WIN_C = 16
D_CONV = 512
CONV_K = 31
D_SGU = 512
SGU_GROUPS = 8
SGU_CHUNK = 128
D_IN = 3 * D_ATTN + 2 * D_CONV + 2 * D_SGU
N_EXPERTS = 32
TOP_K = 4
D_FF = 2048
SWIGLU_LIMIT = 7.0
SWIGLU_ALPHA = 1.702
N_MOD = 6
EPS = 1e-6

SEG = 2048
SEG_ROWS = SEG // GRID_W
HALF = D_MODEL // 2
NEG_BIAS = -1e30
HI_MASK = 0xFFFF0000

V7X_VMEM_LIMIT_BYTES = 56 * 1024 * 1024

TN_ADA = 1024
TM_IN, TN_IN = 1024, 512
TM_SGU = 1024
TM_OUT, TK_OUT = 512, 512
TT_RANK = 512
TM_EXP, TF_EXP = 1024, 256
ROW_CHUNK = TM_EXP // (D_FF // TF_EXP)
TT_COMB = 512
ATTN_ROW_GROUP = 8
CONV_ROWS = 32
HALO = 16


def _cparams(sem):
    return pltpu.CompilerParams(dimension_semantics=sem, vmem_limit_bytes=V7X_VMEM_LIMIT_BYTES)


def _batch_of_tile(i, tile, n_prompt_seg):
    seg = (i * tile) // SEG
    return jnp.where(seg < n_prompt_seg, seg // 2, seg - n_prompt_seg // 2)


def _pack_bf16_pairs(y):
    yb = y.astype(BF16)
    lo = pltpu.bitcast(yb[:, :HALF].astype(F32), U32) >> 16
    hi = pltpu.bitcast(yb[:, HALF:].astype(F32), U32) & jnp.uint32(HI_MASK)
    return lo | hi


def _unpack_lo(u):
    return pltpu.bitcast(u << 16, F32)


def _unpack_hi(u):
    return pltpu.bitcast(u & jnp.uint32(HI_MASK), F32)


def _adaln_kernel(c_ref, w_ref, b_ref, o_ref):
    sc = jax.nn.silu(c_ref[...]).astype(BF16)
    o_ref[0] = jnp.dot(sc, w_ref[0].astype(BF16), preferred_element_type=F32) + b_ref[0]


def _adaln(c_pad, ada_w, ada_b):
    depth = ada_w.shape[0]
    nb = c_pad.shape[0]
    n_out = N_MOD * D_MODEL
    return pl.pallas_call(
        _adaln_kernel,
        grid=(depth, n_out // TN_ADA),
        in_specs=[
            pl.BlockSpec((nb, D_MODEL), lambda l, j: (0, 0)),
            pl.BlockSpec((1, D_MODEL, TN_ADA), lambda l, j: (l, 0, j)),
            pl.BlockSpec((1, 1, TN_ADA), lambda l, j: (l, 0, j)),
        ],
        out_specs=pl.BlockSpec((1, nb, TN_ADA), lambda l, j: (l, 0, j)),
        out_shape=jax.ShapeDtypeStruct((depth, nb, n_out), F32),
        compiler_params=_cparams(("arbitrary", "arbitrary")),
        name="adaln",
    )(c_pad, ada_w, ada_b.reshape(depth, 1, n_out))


def _inproj_kernel(x_ref, nw_ref, shift_ref, scale_ref, w_ref, qn_ref, kn_ref, ones_ref, o_ref, h_ref):
    j = pl.program_id(1)

    @pl.when(j == 0)
    def _():
        xf = x_ref[...]
        y = xf * lax.rsqrt(jnp.mean(xf * xf, axis=-1, keepdims=True) + EPS)
        h = (y * nw_ref[0]) * (1.0 + scale_ref[0]) + shift_ref[0]
        h_ref[...] = h.astype(BF16)

    acc = jnp.dot(h_ref[...], w_ref[0], preferred_element_type=F32)
    n_qk = 2 * D_ATTN // TN_IN

    @pl.when(j < n_qk)
    def _():
        ss = jnp.dot((acc * acc).astype(BF16), ones_ref[...], preferred_element_type=F32)
        r = lax.rsqrt(ss * (1.0 / ATTN_HD) + EPS)
        nw = jnp.where(j < n_qk // 2, qn_ref[...], kn_ref[...])
        o_ref[...] = (acc * r * nw).astype(o_ref.dtype)

    @pl.when(j >= n_qk)
    def _():
        o_ref[...] = acc.astype(o_ref.dtype)


def _inproj(x, mod_rows, norm_w, w_bf16, qn, kn, ones_bd, layer, n_prompt_seg, nb_pad):
    t = x.shape[0]
    depth = norm_w.shape[0]

    def mod_map(k):
        def f(i, j):
            b = _batch_of_tile(i, TM_IN, n_prompt_seg)
            return ((layer * nb_pad + b) * N_MOD + k, 0, 0)
        return f

    return pl.pallas_call(
        _inproj_kernel,
        grid=(t // TM_IN, D_IN // TN_IN),
        in_specs=[
            pl.BlockSpec((TM_IN, D_MODEL), lambda i, j: (i, 0)),
            pl.BlockSpec((1, 1, D_MODEL), lambda i, j: (layer, 0, 0)),
            pl.BlockSpec((1, 1, D_MODEL), mod_map(0)),
            pl.BlockSpec((1, 1, D_MODEL), mod_map(1)),
            pl.BlockSpec((1, D_MODEL, TN_IN), lambda i, j: (layer, 0, j)),
            pl.BlockSpec((1, TN_IN), lambda i, j: (0, 0)),
            pl.BlockSpec((1, TN_IN), lambda i, j: (0, 0)),
            pl.BlockSpec((TN_IN, TN_IN), lambda i, j: (0, 0)),
        ],
        out_specs=pl.BlockSpec((TM_IN, TN_IN), lambda i, j: (i, j)),
        out_shape=jax.ShapeDtypeStruct((t, D_IN), BF16),
        scratch_shapes=[pltpu.VMEM((TM_IN, D_MODEL), BF16)],
        compiler_params=_cparams(("arbitrary", "arbitrary")),
        name="inproj",
    )(x, norm_w.reshape(depth, 1, D_MODEL), mod_rows, mod_rows, w_bf16, qn, kn, ones_bd)


def _attn_kernel(q_ref, k_ref, v_ref, b_ref, o_ref, *, n_prompt_seg):
    seg = pl.program_id(0)
    is_prompt = seg < n_prompt_seg
    half = seg % 2
    rows = jnp.where(is_prompt, 2 * SEG_ROWS, SEG_ROWS)
    row0 = jnp.where(is_prompt, half * SEG_ROWS, 0)
    kv0 = jnp.where(is_prompt, 0, half * SEG)
    lane = lax.broadcasted_iota(I32, (GRID_W, 2 * ATTN_HD), 1)
    first_head = lane < ATTN_HD
    band = WIN_R * GRID_W

    def group_body(gi, carry):
        q0s, k0s, scores = [], [], []
        for g in range(ATTN_ROW_GROUP):
            rr = gi * ATTN_ROW_GROUP + g
            r = row0 + rr
            rs = jnp.clip(r - WIN_R // 2, 0, rows - WIN_R)
            di = r - rs
            q0 = pl.multiple_of(rr * GRID_W, GRID_W)
            k0 = pl.multiple_of(kv0 + rs * GRID_W, GRID_W)
            q = q_ref[pl.ds(q0, GRID_W), :]
            zero = jnp.zeros_like(q)
            qm = jnp.concatenate([jnp.where(first_head, q, zero), jnp.where(first_head, zero, q)], axis=0)
            s = lax.dot_general(qm, k_ref[pl.ds(k0, band), :], (((1,), (1,)), ((), ())),
                                preferred_element_type=F32)
            scores.append(s + b_ref[0, 0, di])
            q0s.append(q0)
            k0s.append(k0)
        probs, inv = [], []
        for s in scores:
            m = jnp.max(s, axis=-1, keepdims=True)
            p = jnp.exp(s - m)
            inv.append(1.0 / jnp.sum(p, axis=-1, keepdims=True))
            probs.append(p.astype(BF16))
        for g in range(ATTN_ROW_GROUP):
            o = jnp.dot(probs[g], v_ref[pl.ds(k0s[g], band), :], preferred_element_type=F32) * inv[g]
            o_ref[pl.ds(q0s[g], GRID_W), :] = jnp.where(first_head, o[:GRID_W], o[GRID_W:]).astype(o_ref.dtype)
        return carry

    lax.fori_loop(0, SEG_ROWS // ATTN_ROW_GROUP, group_body, 0)


def _attention(p, bias_tab, layer, n_prompt_seg):
    t = p.shape[0]
    n_seg = t // SEG
    n_pair = ATTN_HEADS // 2
    lanes = 2 * ATTN_HD
    return pl.pallas_call(
        functools.partial(_attn_kernel, n_prompt_seg=n_prompt_seg),
        grid=(n_seg, n_pair),
        in_specs=[
            pl.BlockSpec((SEG, lanes), lambda s, h: (s, h)),
            pl.BlockSpec((2 * SEG, lanes), lambda s, h: (s // 2, n_pair + h)),
            pl.BlockSpec((2 * SEG, lanes), lambda s, h: (s // 2, 2 * n_pair + h)),
            pl.BlockSpec((1, 1, WIN_R, 2 * GRID_W, WIN_R * GRID_W), lambda s, h: (layer, h, 0, 0, 0)),
        ],
        out_specs=pl.BlockSpec((SEG, lanes), lambda s, h: (s, h)),
        out_shape=jax.ShapeDtypeStruct((t, D_ATTN), BF16),
        compiler_params=_cparams(("arbitrary", "arbitrary")),
        name="attention",
    )(p, p, p, bias_tab)


def _bias_table(rel_pos_bias):
    c = np.arange(GRID_W)
    cs = np.clip(c - WIN_C // 2, 0, GRID_W - WIN_C)
    kc = np.arange(GRID_W)
    valid = (kc[None, :] >= cs[:, None]) & (kc[None, :] < cs[:, None] + WIN_C)
    col_rel = np.clip(kc[None, :] - c[:, None] + (WIN_C - 1), 0, 2 * WIN_C - 2)
    depth = rel_pos_bias.shape[0]
    n_col = 2 * WIN_C - 1
    onehot = jnp.asarray(np.arange(n_col)[:, None] == col_rel.reshape(1, -1), F32)
    cols = jnp.einsum('lhrx,xk->lhrk', rel_pos_bias.astype(F32), onehot, precision=lax.Precision.HIGHEST)
    cols = jnp.where(valid.reshape(1, 1, 1, -1), cols, NEG_BIAS)
    cols = cols.reshape(depth, ATTN_HEADS, 2 * WIN_R - 1, GRID_W, GRID_W)
    tab = jnp.stack([cols[:, :, WIN_R - 1 - di:2 * WIN_R - 1 - di] for di in range(WIN_R)], axis=2)
    tab = jnp.swapaxes(tab, 3, 4).reshape(depth, ATTN_HEADS // 2, 2, WIN_R, GRID_W, WIN_R * GRID_W)
    return jnp.swapaxes(tab, 2, 3).reshape(depth, ATTN_HEADS // 2, WIN_R, 2 * GRID_W, WIN_R * GRID_W)


def _conv_kernel(am_ref, gm_ref, ap_ref, gp_ref, an_ref, gn_ref, w_ref, cb_ref, lw_ref, lb_ref, o_ref, buf_ref,
                 *, n_prompt_seg):
    seg = pl.program_id(0)
    is_prompt = seg < n_prompt_seg
    half = seg % 2
    has_prev = jnp.logical_and(is_prompt, half == 1)
    has_next = jnp.logical_and(is_prompt, half == 0)

    def glu(a_ref, g_ref):
        return a_ref[...].astype(F32) * jax.nn.sigmoid(g_ref[...].astype(F32))

    buf_ref[pl.ds(HALO, SEG), :] = glu(am_ref, gm_ref)
    buf_ref[pl.ds(0, HALO), :] = jnp.where(has_prev, glu(ap_ref, gp_ref), 0.0)
    buf_ref[pl.ds(HALO + SEG, HALO), :] = jnp.where(has_next, glu(an_ref, gn_ref), 0.0)

    w = w_ref[0]
    off = HALO - CONV_K // 2

    def chunk(ci, carry):
        r0 = pl.multiple_of(ci * CONV_ROWS, CONV_ROWS)
        win_rows = CONV_ROWS + 32
        win = buf_ref[pl.ds(r0, win_rows), :]
        acc = jnp.zeros((CONV_ROWS, D_CONV), F32)
        for s in range(8):
            ws = win if s == 0 else pltpu.roll(win, win_rows - s, axis=0)
            for a in range(4):
                k = 8 * a + s - off
                if 0 <= k < CONV_K:
                    acc = acc + ws[8 * a:8 * a + CONV_ROWS, :] * w[k:k + 1, :]
        y = acc + cb_ref[0]
        mu = jnp.mean(y, axis=-1, keepdims=True)
        yc = y - mu
        var = jnp.mean(yc * yc, axis=-1, keepdims=True)
        yn = yc * lax.rsqrt(var + EPS) * lw_ref[0] + lb_ref[0]
        o_ref[pl.ds(r0, CONV_ROWS), :] = jax.nn.silu(yn).astype(o_ref.dtype)
        return carry

    lax.fori_loop(0, SEG // CONV_ROWS, chunk, 0)


def _conv(p, conv_w, conv_b, norm_w, norm_b, layer, n_prompt_seg):
    t = p.shape[0]
    depth = conv_w.shape[0]
    n_seg = t // SEG
    ca = 3 * D_ATTN // D_CONV
    cg = ca + 1
    per_seg = SEG // HALO
    last = t // HALO - 1

    def prev_map(col):
        return lambda s: (jnp.maximum(s * per_seg - 1, 0), col)

    def next_map(col):
        return lambda s: (jnp.minimum((s + 1) * per_seg, last), col)

    vec = lambda: pl.BlockSpec((1, 1, D_CONV), lambda s: (layer, 0, 0))
    return pl.pallas_call(
        functools.partial(_conv_kernel, n_prompt_seg=n_prompt_seg),
        grid=(n_seg,),
        in_specs=[
            pl.BlockSpec((SEG, D_CONV), lambda s: (s, ca)),
            pl.BlockSpec((SEG, D_CONV), lambda s: (s, cg)),
            pl.BlockSpec((HALO, D_CONV), prev_map(ca)),
            pl.BlockSpec((HALO, D_CONV), prev_map(cg)),
            pl.BlockSpec((HALO, D_CONV), next_map(ca)),
            pl.BlockSpec((HALO, D_CONV), next_map(cg)),
            pl.BlockSpec((1, CONV_K, D_CONV), lambda s: (layer, 0, 0)),
            vec(), vec(), vec(),
        ],
        out_specs=pl.BlockSpec((SEG, D_CONV), lambda s: (s, 0)),
        out_shape=jax.ShapeDtypeStruct((t, D_CONV), BF16),
        scratch_shapes=[pltpu.VMEM((SEG + 2 * HALO, D_CONV), F32)],
        compiler_params=_cparams(("arbitrary",)),
        name="conv",
    )(p, p, p, p, p, p, conv_w, conv_b.reshape(depth, 1, D_CONV), norm_w.reshape(depth, 1, D_CONV),
      norm_b.reshape(depth, 1, D_CONV))


def _sgu_kernel(u_ref, v_ref, lw_ref, lb_ref, ws_ref, bs_ref, o_ref):
    u = jax.nn.gelu(u_ref[...].astype(F32))
    g = jax.nn.gelu(v_ref[...].astype(F32))
    mu = jnp.mean(g, axis=-1, keepdims=True)
    gc = g - mu
    var = jnp.mean(gc * gc, axis=-1, keepdims=True)
    v = (gc * lax.rsqrt(var + EPS) * lw_ref[0] + lb_ref[0]).astype(BF16)
    lanes = 2 * (D_SGU // SGU_GROUPS)
    first_group = lax.broadcasted_iota(I32, (SGU_CHUNK, lanes), 1) < lanes // 2
    for c in range(TM_SGU // SGU_CHUNK):
        rows = slice(c * SGU_CHUNK, (c + 1) * SGU_CHUNK)
        for pair in range(SGU_GROUPS // 2):
            cols = slice(pair * lanes, (pair + 1) * lanes)
            vp = v[rows, cols]
            m0 = jnp.dot(ws_ref[0, 2 * pair].astype(BF16), vp, preferred_element_type=F32)
            m1 = jnp.dot(ws_ref[0, 2 * pair + 1].astype(BF16), vp, preferred_element_type=F32)
            mixed = jnp.where(first_group, m0, m1) + bs_ref[:, cols]
            o_ref[rows, cols] = (u[rows, cols] * mixed).astype(o_ref.dtype)


def _sgu(p, norm_w, norm_b, sgu_w, bias_exp, layer):
    t = p.shape[0]
    depth = norm_w.shape[0]
    su = (3 * D_ATTN + 2 * D_CONV) // D_SGU
    sv = su + 1
    vec = lambda: pl.BlockSpec((1, 1, D_SGU), lambda i: (layer, 0, 0))
    return pl.pallas_call(
        _sgu_kernel,
        grid=(t // TM_SGU,),
        in_specs=[
            pl.BlockSpec((TM_SGU, D_SGU), lambda i: (i, su)),
            pl.BlockSpec((TM_SGU, D_SGU), lambda i: (i, sv)),
            vec(), vec(),
            pl.BlockSpec((1, SGU_GROUPS, SGU_CHUNK, SGU_CHUNK), lambda i: (layer, 0, 0, 0)),
            pl.BlockSpec((SGU_CHUNK, D_SGU), lambda i: (0, 0)),
        ],
        out_specs=pl.BlockSpec((TM_SGU, D_SGU), lambda i: (i, 0)),
        out_shape=jax.ShapeDtypeStruct((t, D_SGU), BF16),
        compiler_params=_cparams(("arbitrary",)),
        name="sgu",
    )(p, p, norm_w.reshape(depth, 1, D_SGU), norm_b.reshape(depth, 1, D_SGU), sgu_w, bias_exp)


def _rms(y, w):
    return y * lax.rsqrt(jnp.mean(y * y, axis=-1, keepdims=True) + EPS) * w


def _outproj_kernel(ya_ref, yc_ref, ys_ref, x_ref, mw_ref, g1_ref, nw_ref, sh2_ref, sc2_ref, w_ref, rwt_ref, rb_ref,
                    x1_ref, hp_ref, ti_ref, gcol_ref, mix_ref, acc_ref):
    k = pl.program_id(1)
    nk = D_MODEL // TK_OUT

    @pl.when(k == 0)
    def _():
        mw = mw_ref[0]
        na = _rms(ya_ref[...].astype(F32), mw[:, :D_ATTN])
        nc = _rms(yc_ref[...].astype(F32), mw[:, D_ATTN:D_ATTN + D_CONV])
        ns = _rms(ys_ref[...].astype(F32), mw[:, D_ATTN + D_CONV:])
        mix_ref[0] = na[:, :TK_OUT].astype(BF16)
        mix_ref[1] = na[:, TK_OUT:].astype(BF16)
        mix_ref[2] = nc.astype(BF16)
        mix_ref[3] = ns.astype(BF16)
        acc_ref[...] = jnp.zeros_like(acc_ref)

    acc_ref[...] += jnp.dot(mix_ref[k], w_ref[0], preferred_element_type=F32)

    @pl.when(k == nk - 1)
    def _():
        x1 = x_ref[...] + g1_ref[0] * acc_ref[...]
        x1_ref[...] = x1
        h2 = _rms(x1, nw_ref[0]) * (1.0 + sc2_ref[0]) + sh2_ref[0]
        hp_ref[...] = _pack_bf16_pairs(h2)

        hi = h2.astype(BF16)
        lo = (h2 - hi.astype(F32)).astype(BF16)
        rw = rwt_ref[0]
        rwh = rw.astype(BF16)
        rwl = (rw - rwh.astype(F32)).astype(BF16)
        dn = (((1,), (1,)), ((), ()))
        logits = (lax.dot_general(rwh, hi, dn, preferred_element_type=F32)
                  + lax.dot_general(rwh, lo, dn, preferred_element_type=F32)
                  + lax.dot_general(rwl, hi, dn, preferred_element_type=F32)) + rb_ref[0]

        tm = logits.shape[1]
        ie = lax.broadcasted_iota(I32, (N_EXPERTS, tm), 0)
        work = logits
        vals, idxs = [], []
        for _ in range(TOP_K):
            m = jnp.max(work, axis=0, keepdims=True)
            idx = jnp.min(jnp.where(work == m, ie, N_EXPERTS), axis=0, keepdims=True)
            vals.append(m)
            idxs.append(idx)
            work = jnp.where(ie == idx, -jnp.inf, work)
        ex = [jnp.exp(v - vals[0]) for v in vals]
        den = ex[0] + ex[1] + ex[2] + ex[3]
        ti_ref[...] = jnp.concatenate(idxs, axis=0)
        gates = jnp.concatenate([e / den for e in ex] + [jnp.zeros((128 - TOP_K, tm), F32)], axis=0)
        gcol_ref[...] = gates.T


def _outproj(ya, yc, ys, x, mod_rows, mix_w, ffn_w, w_bf16, rwt, rb, layer, n_prompt_seg, nb_pad):
    t = x.shape[0]
    depth = mix_w.shape[0]

    def mod_map(k):
        def f(i, kk):
            b = _batch_of_tile(i, TM_OUT, n_prompt_seg)
            return ((layer * nb_pad + b) * N_MOD + k, 0, 0)
        return f

    lvec = lambda: pl.BlockSpec((1, 1, D_MODEL), lambda i, k: (layer, 0, 0))
    return pl.pallas_call(
        _outproj_kernel,
        grid=(t // TM_OUT, D_MODEL // TK_OUT),
        in_specs=[
            pl.BlockSpec((TM_OUT, D_ATTN), lambda i, k: (i, 0)),
            pl.BlockSpec((TM_OUT, D_CONV), lambda i, k: (i, 0)),
            pl.BlockSpec((TM_OUT, D_SGU), lambda i, k: (i, 0)),
            pl.BlockSpec((TM_OUT, D_MODEL), lambda i, k: (i, 0)),
            lvec(),
            pl.BlockSpec((1, 1, D_MODEL), mod_map(2)),
            lvec(),
            pl.BlockSpec((1, 1, D_MODEL), mod_map(3)),
            pl.BlockSpec((1, 1, D_MODEL), mod_map(4)),
            pl.BlockSpec((1, TK_OUT, D_MODEL), lambda i, k: (layer, k, 0)),
            pl.BlockSpec((1, N_EXPERTS, D_MODEL), lambda i, k: (layer, 0, 0)),
            pl.BlockSpec((1, N_EXPERTS, 1), lambda i, k: (layer, 0, 0)),
        ],
        out_specs=[
            pl.BlockSpec((TM_OUT, D_MODEL), lambda i, k: (i, 0)),
            pl.BlockSpec((TM_OUT, HALF), lambda i, k: (i, 0)),
            pl.BlockSpec((TOP_K, TM_OUT), lambda i, k: (0, i)),
            pl.BlockSpec((TM_OUT, 128), lambda i, k: (i, 0)),
        ],
        out_shape=[
            jax.ShapeDtypeStruct((t, D_MODEL), F32),
            jax.ShapeDtypeStruct((t, HALF), U32),
            jax.ShapeDtypeStruct((TOP_K, t), I32),
            jax.ShapeDtypeStruct((t, 128), F32),
        ],
        scratch_shapes=[
            pltpu.VMEM((D_MODEL // TK_OUT, TM_OUT, TK_OUT), BF16),
            pltpu.VMEM((TM_OUT, D_MODEL), F32),
        ],
        compiler_params=_cparams(("arbitrary", "arbitrary")),
        name="outproj",
    )(ya, yc, ys, x, mix_w.reshape(depth, 1, D_MODEL), mod_rows, ffn_w.reshape(depth, 1, D_MODEL), mod_rows,
      mod_rows, w_bf16, rwt, rb.reshape(depth, N_EXPERTS, 1))


def _rank_kernel(ti_ref, tri_ref, rank_ref, cnt_ref, base_ref):
    i = pl.program_id(0)

    @pl.when(i == 0)
    def _():
        base_ref[...] = jnp.zeros_like(base_ref)

    e = ti_ref[...]
    tt = e.shape[1]
    ie = lax.broadcasted_iota(I32, (N_EXPERTS, tt), 0)
    pre = base_ref[:, 0:1]
    ranks = []
    for s in range(TOP_K):
        hit = ie == e[s:s + 1, :]
        hf = hit.astype(F32)
        earlier = jnp.dot(hf.astype(BF16), tri_ref[...], preferred_element_type=F32)
        ranks.append(jnp.sum(jnp.where(hit, pre + earlier, 0.0), axis=0, keepdims=True))
        pre = pre + jnp.sum(hf, axis=1, keepdims=True)
    rank_ref[...] = jnp.concatenate(ranks, axis=0).astype(I32)
    total = jnp.broadcast_to(pre, base_ref.shape)
    base_ref[...] = total
    cnt_ref[...] = total.astype(I32)


def _rank(ti, tri):
    t = ti.shape[1]
    return pl.pallas_call(
        _rank_kernel,
        grid=(t // TT_RANK,),
        in_specs=[
            pl.BlockSpec((TOP_K, TT_RANK), lambda i: (0, i)),
            pl.BlockSpec((TT_RANK, TT_RANK), lambda i: (0, 0)),
        ],
        out_specs=[
            pl.BlockSpec((TOP_K, TT_RANK), lambda i: (0, i)),
            pl.BlockSpec((N_EXPERTS, 128), lambda i: (0, 0)),
        ],
        out_shape=[
            jax.ShapeDtypeStruct((TOP_K, t), I32),
            jax.ShapeDtypeStruct((N_EXPERTS, 128), I32),
        ],
        scratch_shapes=[pltpu.VMEM((N_EXPERTS, 128), F32)],
        compiler_params=_cparams(("arbitrary",)),
        name="rank",
    )(ti, tri)


def _expert_kernel(be_ref, nu_ref, tok0_ref, tokn_ref, dstp_ref, hp_ref, wg_ref, wu_ref, bg_ref, bu_ref, wd_ref, bd_ref,
                   yt_ref, xg_ref, ys_ref, xb_ref, acc_ref, gsem, ssem):
    i = pl.program_id(0)
    j = pl.program_id(1)
    nf = D_FF // TF_EXP
    nused = nu_ref[0]
    active = i < nused
    in_flight = i <= nused

    def gather_copy(par, rr, tok):
        return pltpu.make_async_copy(hp_ref.at[pl.ds(tok, 1)], xg_ref.at[par, pl.ds(rr, 1)], gsem.at[par])

    def scatter_copy(par, rr, dst):
        return pltpu.make_async_copy(ys_ref.at[par, pl.ds(rr, 1)], yt_ref.at[pl.ds(dst, 1)], ssem)

    def issue_gather(tok_ref, par, chunk):
        base = pl.multiple_of(chunk * ROW_CHUNK, ROW_CHUNK)
        for r in range(ROW_CHUNK):
            gather_copy(par, base + r, tok_ref[0, 0, base + r]).start()

    def issue_scatter(par, chunk):
        base = pl.multiple_of(chunk * ROW_CHUNK, ROW_CHUNK)
        for r in range(ROW_CHUNK):
            scatter_copy(par, base + r, dstp_ref[0, 0, base + r]).start()

    def wait_rows(copy):
        def body(r, c):
            copy.wait()
            return c
        lax.fori_loop(0, TM_EXP, body, 0, unroll=8)

    @pl.when(jnp.logical_and(i == 0, j == 0))
    def _():
        ys_ref[1] = jnp.zeros((TM_EXP, HALF), U32)

        def first_block(chunk, c):
            issue_gather(tok0_ref, 0, chunk)
            return c

        lax.fori_loop(0, nf, first_block, 0)

    @pl.when(jnp.logical_and(in_flight, j == 0))
    def _():
        wait_rows(gather_copy(lax.rem(i, 2), 0, 0))

    @pl.when(jnp.logical_and(active, j == 0))
    def _():
        u = xg_ref[lax.rem(i, 2)]
        xb_ref[:, :HALF] = _unpack_lo(u).astype(BF16)
        xb_ref[:, HALF:] = _unpack_hi(u).astype(BF16)
        acc_ref[...] = jnp.zeros_like(acc_ref)

    @pl.when(i == nused)
    def _():
        issue_scatter(lax.rem(i + 1, 2), j)

    @pl.when(active)
    def _():
        issue_gather(tokn_ref, lax.rem(i + 1, 2), j)
        issue_scatter(lax.rem(i + 1, 2), j)
        xb = xb_ref[...]
        gate = jnp.dot(xb, wg_ref[0, 0].astype(BF16), preferred_element_type=F32) + bg_ref[0, 0]
        up = jnp.dot(xb, wu_ref[0, 0].astype(BF16), preferred_element_type=F32) + bu_ref[0, 0]
        gate = jnp.minimum(gate, SWIGLU_LIMIT)
        up = jnp.clip(up, -SWIGLU_LIMIT, SWIGLU_LIMIT)
        act = (up + 1.0) * (gate * jax.nn.sigmoid(SWIGLU_ALPHA * gate))
        acc_ref[...] += jnp.dot(act.astype(BF16), wd_ref[0, 0].astype(BF16), preferred_element_type=F32)

    @pl.when(jnp.logical_and(in_flight, j == nf - 1))
    def _():
        wait_rows(scatter_copy(0, 0, 0))

    @pl.when(jnp.logical_and(active, j == nf - 1))
    def _():
        ys_ref[lax.rem(i, 2)] = _pack_bf16_pairs(acc_ref[...] + bd_ref[0, 0])


def _experts(blk_e, n_used, tok_rows, dst_rows, hp, w_gu, b_gu, w_dn, b_dn, layer):
    n_tok = hp.shape[0]
    n_blocks = blk_e.shape[0]
    depth = w_gu.shape[0]
    nf = D_FF // TF_EXP

    def blk(i, nu):
        return jnp.minimum(i, nu[0] - 1)

    def ftile(i, j, nu):
        return jnp.where(i < nu[0], j, nf - 1)

    def rows_spec(index):
        return pl.BlockSpec((1, 1, TM_EXP), lambda i, j, be, nu: (index(i), 0, 0), memory_space=pltpu.SMEM)

    grid_spec = pltpu.PrefetchScalarGridSpec(
        num_scalar_prefetch=2,
        grid=(n_blocks + 1, nf),
        in_specs=[
            rows_spec(lambda i: 1),
            rows_spec(lambda i: jnp.minimum(i + 2, n_blocks + 1)),
            rows_spec(lambda i: i),
            pl.BlockSpec(memory_space=pl.ANY),
            pl.BlockSpec((1, 1, D_MODEL, TF_EXP), lambda i, j, be, nu: (layer, be[blk(i, nu)], 0, ftile(i, j, nu))),
            pl.BlockSpec((1, 1, D_MODEL, TF_EXP),
                         lambda i, j, be, nu: (layer, be[blk(i, nu)], 0, nf + ftile(i, j, nu))),
            pl.BlockSpec((1, 1, 1, TF_EXP), lambda i, j, be, nu: (layer, be[blk(i, nu)], 0, ftile(i, j, nu))),
            pl.BlockSpec((1, 1, 1, TF_EXP), lambda i, j, be, nu: (layer, be[blk(i, nu)], 0, nf + ftile(i, j, nu))),
            pl.BlockSpec((1, 1, TF_EXP, D_MODEL), lambda i, j, be, nu: (layer, be[blk(i, nu)], ftile(i, j, nu), 0)),
            pl.BlockSpec((1, 1, 1, D_MODEL), lambda i, j, be, nu: (layer, be[blk(i, nu)], 0, 0)),
        ],
        out_specs=pl.BlockSpec(memory_space=pl.ANY),
        scratch_shapes=[
            pltpu.VMEM((2, TM_EXP, HALF), U32),
            pltpu.VMEM((2, TM_EXP, HALF), U32),
            pltpu.VMEM((TM_EXP, D_MODEL), BF16),
            pltpu.VMEM((TM_EXP, D_MODEL), F32),
            pltpu.SemaphoreType.DMA((2,)),
            pltpu.SemaphoreType.DMA(()),
        ],
    )
    return pl.pallas_call(
        _expert_kernel,
        grid_spec=grid_spec,
        out_shape=jax.ShapeDtypeStruct((TOP_K * n_tok + 2 * TM_EXP, HALF), U32),
        compiler_params=_cparams(("arbitrary", "arbitrary")),
        name="experts",
    )(blk_e, n_used, tok_rows, tok_rows, dst_rows, hp, w_gu, w_gu, b_gu.reshape(depth, N_EXPERTS, 1, 2 * D_FF),
      b_gu.reshape(depth, N_EXPERTS, 1, 2 * D_FF), w_dn, b_dn.reshape(depth, N_EXPERTS, 1, D_MODEL))


def _combine_kernel(gcol_ref, x_ref, g2_ref, y0_ref, y1_ref, y2_ref, y3_ref, o_ref):
    g = gcol_ref[...]
    lo = jnp.zeros((TT_COMB, HALF), F32)
    hi = jnp.zeros((TT_COMB, HALF), F32)
    for s, y_ref in enumerate((y0_ref, y1_ref, y2_ref, y3_ref)):
        u = y_ref[...]
        gs = g[:, s:s + 1]
        lo = lo + _unpack_lo(u) * gs
        hi = hi + _unpack_hi(u) * gs
    g2 = g2_ref[0]
    o_ref[:, :HALF] = x_ref[:, :HALF] + g2[:, :HALF] * lo
    o_ref[:, HALF:] = x_ref[:, HALF:] + g2[:, HALF:] * hi


def _combine(gcol, x1, mod_rows, yt, layer, n_prompt_seg, nb_pad):
    t = x1.shape[0]
    per_slot = t // TT_COMB

    def g2_map(i):
        b = _batch_of_tile(i, TT_COMB, n_prompt_seg)
        return ((layer * nb_pad + b) * N_MOD + 5, 0, 0)

    def slot_spec(s):
        return pl.BlockSpec((TT_COMB, HALF), lambda i: (s * per_slot + i, 0))

    return pl.pallas_call(
        _combine_kernel,
        grid=(per_slot,),
        in_specs=[
            pl.BlockSpec((TT_COMB, 128), lambda i: (i, 0)),
            pl.BlockSpec((TT_COMB, D_MODEL), lambda i: (i, 0)),
            pl.BlockSpec((1, 1, D_MODEL), g2_map),
            slot_spec(0), slot_spec(1), slot_spec(2), slot_spec(3),
        ],
        out_specs=pl.BlockSpec((TT_COMB, D_MODEL), lambda i: (i, 0)),
        out_shape=jax.ShapeDtypeStruct((t, D_MODEL), F32),
        compiler_params=_cparams(("arbitrary",)),
        name="combine",
    )(gcol, x1, mod_rows, yt, yt, yt, yt)


def _routing_plan(ti, rank, counts, n_blocks):
    pcounts = (counts + TM_EXP - 1) // TM_EXP * TM_EXP
    pend = jnp.cumsum(pcounts)
    pstart = pend - pcounts
    cstart = jnp.cumsum(counts) - counts
    experts = jnp.arange(N_EXPERTS, dtype=I32)
    pos = rank + jnp.sum(jnp.where(ti[None] == experts[:, None, None], cstart[:, None, None], 0), axis=0)
    n_tok = ti.shape[1]
    assert n_tok & (n_tok - 1) == 0, "the token index is taken from the slot id by masking"
    order = jnp.argsort(pos.reshape(-1)).astype(I32)
    order = jnp.concatenate([order, jnp.zeros((TM_EXP,), I32)])
    first_row = jnp.arange(n_blocks, dtype=I32) * TM_EXP
    blk_e = jnp.minimum(jnp.sum((pend[None, :] <= first_row[:, None]).astype(I32), axis=1), N_EXPERTS - 1)
    n_used = (pend[-1:] // TM_EXP).astype(I32)
    in_expert = first_row - pstart[blk_e]
    nvalid = jnp.clip(counts[blk_e] - in_expert, 0, TM_EXP)
    koff = jnp.where(nvalid > 0, cstart[blk_e] + in_expert, 0)
    zero = jnp.zeros((1,), I32)
    koff = jnp.concatenate([zero, koff.astype(I32), zero])
    nvalid = jnp.concatenate([zero, nvalid.astype(I32), zero])
    slots = jax.vmap(lambda k: lax.dynamic_slice(order, (k,), (TM_EXP,)))(koff)
    row = jnp.arange(TM_EXP, dtype=I32)[None, :]
    valid = row < nvalid[:, None]
    parity = (jnp.arange(n_blocks + 2, dtype=I32)[:, None] + 1) % 2
    tok_rows = jnp.where(valid, jnp.bitwise_and(slots, n_tok - 1), 0)
    dst_rows = jnp.where(valid, slots, TOP_K * n_tok + parity * TM_EXP + row)
    shape = (n_blocks + 2, 1, TM_EXP)
    return blk_e, n_used, tok_rows.reshape(shape), dst_rows.reshape(shape)


def kernel(x_prompt, x_sample, c_prompt, c_sample, ada_w, ada_b, norm_mix_w, norm_ffn_w, w_in, q_norm_w, k_norm_w, rel_pos_bias, conv_w, conv_b, conv_norm_w, conv_norm_b, sgu_norm_w, sgu_norm_b, sgu_w, sgu_b, mix_norm_w, w_out, router_w, router_b, w_gate_up, b_gate_up, w_down, b_down):
    bp, n_p, d = x_prompt.shape
    bs, n_s, _ = x_sample.shape
    assert d == D_MODEL and n_p == 2 * SEG and n_s == SEG
    depth = ada_w.shape[0]
    n_prompt_seg = 2 * bp
    t = bp * n_p + bs * n_s
    nb = bp + bs
    nb_pad = -(-nb // 8) * 8

    x = jnp.concatenate([x_prompt.reshape(bp * n_p, d), x_sample.reshape(bs * n_s, d)], axis=0)
    c_pad = jnp.concatenate([c_prompt, c_sample, jnp.zeros((nb_pad - nb, d), F32)], axis=0)
    mod_rows = _adaln(c_pad, ada_w, ada_b).reshape(depth * nb_pad * N_MOD, 1, D_MODEL)

    w_in_b = w_in.astype(BF16)
    w_out_b = w_out.astype(BF16)
    rwt = jnp.swapaxes(router_w, 1, 2)
    bias_tab = _bias_table(rel_pos_bias)
    group = np.arange(TN_IN) // ATTN_HD
    ones_bd = jnp.asarray(group[:, None] == group[None, :], BF16)
    tri = jnp.asarray(np.arange(TT_RANK)[:, None] < np.arange(TT_RANK)[None, :], BF16)
    n_slot = t * TOP_K
    n_blocks = -(-(n_slot + N_EXPERTS * (TM_EXP - 1)) // TM_EXP)

    for l in range(depth):
        qn = jnp.tile(q_norm_w[l] * (ATTN_HD ** -0.5), TN_IN // ATTN_HD).reshape(1, TN_IN)
        kn = jnp.tile(k_norm_w[l], TN_IN // ATTN_HD).reshape(1, TN_IN)
        sgu_bias = jnp.repeat(sgu_b[l].T, D_SGU // SGU_GROUPS, axis=1)

        p = _inproj(x, mod_rows, norm_mix_w, w_in_b, qn, kn, ones_bd, l, n_prompt_seg, nb_pad)
        ya = _attention(p, bias_tab, l, n_prompt_seg)
        yc = _conv(p, conv_w, conv_b, conv_norm_w, conv_norm_b, l, n_prompt_seg)
        ys = _sgu(p, sgu_norm_w, sgu_norm_b, sgu_w, sgu_bias, l)
        x1, hp, ti, gcol = _outproj(ya, yc, ys, x, mod_rows, mix_norm_w, norm_ffn_w, w_out_b, rwt, router_b, l,
                                    n_prompt_seg, nb_pad)
        rank, cnt = _rank(ti, tri)
        blk_e, n_used, tok_rows, dst_rows = _routing_plan(ti, rank, cnt[:, 0], n_blocks)
        yt = _experts(blk_e, n_used, tok_rows, dst_rows, hp, w_gate_up, b_gate_up, w_down, b_down, l)
        x = _combine(gcol, x1, mod_rows, yt, l, n_prompt_seg, nb_pad)

    y_prompt = x[:bp * n_p].reshape(bp, n_p, d)
    y_sample = x[bp * n_p:].reshape(bs, n_s, d)
    return (y_prompt, y_sample)
```

```python
import functools

import numpy as np
import jax
import jax.numpy as jnp
from jax import lax
from jax.experimental import pallas as pl
from jax.experimental.pallas import tpu as pltpu

F32 = jnp.float32
BF16 = jnp.bfloat16
U32 = jnp.uint32
I32 = jnp.int32

D_MODEL = 2048
GRID_W = 64
ATTN_HEADS = 16
ATTN_HD = 64
D_ATTN = ATTN_HEADS * ATTN_HD
WIN_R = 8
WIN_C = 16
D_CONV = 512
CONV_K = 31
D_SGU = 512
SGU_GROUPS = 8
SGU_CHUNK = 128
D_IN = 3 * D_ATTN + 2 * D_CONV + 2 * D_SGU
N_EXPERTS = 32
TOP_K = 4
D_FF = 2048
SWIGLU_LIMIT = 7.0
SWIGLU_ALPHA = 1.702
N_MOD = 6
EPS = 1e-6

SEG = 2048
SEG_ROWS = SEG // GRID_W
HALF = D_MODEL // 2
NEG_BIAS = -1e30
HI_MASK = 0xFFFF0000

V7X_VMEM_LIMIT_BYTES = 56 * 1024 * 1024

TN_ADA = 1024
TM_IN, TN_IN = 1024, 512
TM_SGU = 1024
TM_OUT, TK_OUT = 512, 1024
TT_RANK = 512
TT_DISP = 1024
TM_EXP, TF_EXP = 1024, 256
ROW_STEP = 128
TT_COMB = 512
ATTN_ROW_GROUP = 16
CONV_ROWS = 128
HALO = 16


def _cparams(sem):
    return pltpu.CompilerParams(dimension_semantics=sem, vmem_limit_bytes=V7X_VMEM_LIMIT_BYTES)


def _batch_of_tile(i, tile, n_prompt_seg):
    seg = (i * tile) // SEG
    return jnp.where(seg < n_prompt_seg, seg // 2, seg - n_prompt_seg // 2)


def _pack_bf16_pairs(y):
    yb = y.astype(BF16)
    lo = pltpu.bitcast(yb[:, :HALF].astype(F32), U32) >> 16
    hi = pltpu.bitcast(yb[:, HALF:].astype(F32), U32) & jnp.uint32(HI_MASK)
    return lo | hi


def _unpack_lo(u):
    return pltpu.bitcast(u << 16, F32)


def _unpack_hi(u):
    return pltpu.bitcast(u & jnp.uint32(HI_MASK), F32)


def _adaln_kernel(c_ref, w_ref, b_ref, o_ref):
    sc = jax.nn.silu(c_ref[...]).astype(BF16)
    o_ref[0] = jnp.dot(sc, w_ref[0].astype(BF16), preferred_element_type=F32) + b_ref[0]


def _adaln(c_pad, ada_w, ada_b):
    depth = ada_w.shape[0]
    nb = c_pad.shape[0]
    n_out = N_MOD * D_MODEL
    return pl.pallas_call(
        _adaln_kernel,
        grid=(depth, n_out // TN_ADA),
        in_specs=[
            pl.BlockSpec((nb, D_MODEL), lambda l, j: (0, 0)),
            pl.BlockSpec((1, D_MODEL, TN_ADA), lambda l, j: (l, 0, j)),
            pl.BlockSpec((1, 1, TN_ADA), lambda l, j: (l, 0, j)),
        ],
        out_specs=pl.BlockSpec((1, nb, TN_ADA), lambda l, j: (l, 0, j)),
        out_shape=jax.ShapeDtypeStruct((depth, nb, n_out), F32),
        compiler_params=_cparams(("arbitrary", "arbitrary")),
        name="adaln",
    )(c_pad, ada_w, ada_b.reshape(depth, 1, n_out))


def _inproj_kernel(x_ref, nw_ref, shift_ref, scale_ref, w_ref, qn_ref, kn_ref, ones_ref, o_ref, h_ref):
    j = pl.program_id(1)

    @pl.when(j == 0)
    def _():
        xf = x_ref[...]
        y = xf * lax.rsqrt(jnp.mean(xf * xf, axis=-1, keepdims=True) + EPS)
        h = (y * nw_ref[0]) * (1.0 + scale_ref[0]) + shift_ref[0]
        h_ref[...] = h.astype(BF16)

    acc = jnp.dot(h_ref[...], w_ref[0], preferred_element_type=F32)
    n_qk = 2 * D_ATTN // TN_IN

    @pl.when(j < n_qk)
    def _():
        ss = jnp.dot((acc * acc).astype(BF16), ones_ref[...], preferred_element_type=F32)
        r = lax.rsqrt(ss * (1.0 / ATTN_HD) + EPS)
        nw = jnp.where(j < n_qk // 2, qn_ref[...], kn_ref[...])
        o_ref[...] = (acc * r * nw).astype(o_ref.dtype)

    @pl.when(j >= n_qk)
    def _():
        o_ref[...] = acc.astype(o_ref.dtype)


def _inproj(x, mod_rows, norm_w, w_bf16, qn, kn, ones_bd, layer, n_prompt_seg, nb_pad):
    t = x.shape[0]
    depth = norm_w.shape[0]

    def mod_map(k):
        def f(i, j):
            b = _batch_of_tile(i, TM_IN, n_prompt_seg)
            return ((layer * nb_pad + b) * N_MOD + k, 0, 0)
        return f

    return pl.pallas_call(
        _inproj_kernel,
        grid=(t // TM_IN, D_IN // TN_IN),
        in_specs=[
            pl.BlockSpec((TM_IN, D_MODEL), lambda i, j: (i, 0)),
            pl.BlockSpec((1, 1, D_MODEL), lambda i, j: (layer, 0, 0)),
            pl.BlockSpec((1, 1, D_MODEL), mod_map(0)),
            pl.BlockSpec((1, 1, D_MODEL), mod_map(1)),
            pl.BlockSpec((1, D_MODEL, TN_IN), lambda i, j: (layer, 0, j)),
            pl.BlockSpec((1, TN_IN), lambda i, j: (0, 0)),
            pl.BlockSpec((1, TN_IN), lambda i, j: (0, 0)),
            pl.BlockSpec((TN_IN, TN_IN), lambda i, j: (0, 0)),
        ],
        out_specs=pl.BlockSpec((TM_IN, TN_IN), lambda i, j: (i, j)),
        out_shape=jax.ShapeDtypeStruct((t, D_IN), BF16),
        scratch_shapes=[pltpu.VMEM((TM_IN, D_MODEL), BF16)],
        compiler_params=_cparams(("arbitrary", "arbitrary")),
        name="inproj",
    )(x, norm_w.reshape(depth, 1, D_MODEL), mod_rows, mod_rows, w_bf16, qn, kn, ones_bd)


def _attn_kernel(q_ref, k_ref, v_ref, b_ref, o_ref, *, n_prompt_seg):
    seg = pl.program_id(0)
    is_prompt = seg < n_prompt_seg
    half = seg % 2
    rows = jnp.where(is_prompt, 2 * SEG_ROWS, SEG_ROWS)
    row0 = jnp.where(is_prompt, half * SEG_ROWS, 0)
    kv0 = jnp.where(is_prompt, 0, half * SEG)
    lane = lax.broadcasted_iota(I32, (GRID_W, 2 * ATTN_HD), 1)
    first_head = lane < ATTN_HD
    band = WIN_R * GRID_W

    def group_body(gi, carry):
        q0s, k0s, scores = [], [], []
        for g in range(ATTN_ROW_GROUP):
            rr = gi * ATTN_ROW_GROUP + g
            r = row0 + rr
            rs = jnp.clip(r - WIN_R // 2, 0, rows - WIN_R)
            di = r - rs
            q0 = pl.multiple_of(rr * GRID_W, GRID_W)
            k0 = pl.multiple_of(kv0 + rs * GRID_W, GRID_W)
            q = q_ref[pl.ds(q0, GRID_W), :]
            zero = jnp.zeros_like(q)
            qm = jnp.concatenate([jnp.where(first_head, q, zero), jnp.where(first_head, zero, q)], axis=0)
            s = lax.dot_general(qm, k_ref[pl.ds(k0, band), :], (((1,), (1,)), ((), ())),
                                preferred_element_type=F32)
            scores.append(s + b_ref[0, 0, di])
            q0s.append(q0)
            k0s.append(k0)
        probs, inv = [], []
        for s in scores:
            m = jnp.max(s, axis=-1, keepdims=True)
            p = jnp.exp(s - m)
            inv.append(1.0 / jnp.sum(p, axis=-1, keepdims=True))
            probs.append(p.astype(BF16))
        for g in range(ATTN_ROW_GROUP):
            o = jnp.dot(probs[g], v_ref[pl.ds(k0s[g], band), :], preferred_element_type=F32) * inv[g]
            o_ref[pl.ds(q0s[g], GRID_W), :] = jnp.where(first_head, o[:GRID_W], o[GRID_W:]).astype(o_ref.dtype)
        return carry

    lax.fori_loop(0, SEG_ROWS // ATTN_ROW_GROUP, group_body, 0)


def _attention(p, bias_tab, layer, n_prompt_seg):
    t = p.shape[0]
    n_seg = t // SEG
    n_pair = ATTN_HEADS // 2
    lanes = 2 * ATTN_HD
    return pl.pallas_call(
        functools.partial(_attn_kernel, n_prompt_seg=n_prompt_seg),
        grid=(n_seg, n_pair),
        in_specs=[
            pl.BlockSpec((SEG, lanes), lambda s, h: (s, h)),
            pl.BlockSpec((2 * SEG, lanes), lambda s, h: (s // 2, n_pair + h)),
            pl.BlockSpec((2 * SEG, lanes), lambda s, h: (s // 2, 2 * n_pair + h)),
            pl.BlockSpec((1, 1, WIN_R, 2 * GRID_W, WIN_R * GRID_W), lambda s, h: (layer, h, 0, 0, 0)),
        ],
        out_specs=pl.BlockSpec((SEG, lanes), lambda s, h: (s, h)),
        out_shape=jax.ShapeDtypeStruct((t, D_ATTN), BF16),
        compiler_params=_cparams(("arbitrary", "arbitrary")),
        name="attention",
    )(p, p, p, bias_tab)


def _bias_table(rel_pos_bias):
    c = np.arange(GRID_W)
    cs = np.clip(c - WIN_C // 2, 0, GRID_W - WIN_C)
    kc = np.arange(GRID_W)
    valid = (kc[None, :] >= cs[:, None]) & (kc[None, :] < cs[:, None] + WIN_C)
    col_rel = np.clip(kc[None, :] - c[:, None] + (WIN_C - 1), 0, 2 * WIN_C - 2)
    depth = rel_pos_bias.shape[0]
    n_col = 2 * WIN_C - 1
    onehot = jnp.asarray(np.arange(n_col)[:, None] == col_rel.reshape(1, -1), F32)
    cols = jnp.einsum('lhrx,xk->lhrk', rel_pos_bias.astype(F32), onehot, precision=lax.Precision.HIGHEST)
    cols = jnp.where(valid.reshape(1, 1, 1, -1), cols, NEG_BIAS)
    cols = cols.reshape(depth, ATTN_HEADS, 2 * WIN_R - 1, GRID_W, GRID_W)
    tab = jnp.stack([cols[:, :, WIN_R - 1 - di:2 * WIN_R - 1 - di] for di in range(WIN_R)], axis=2)
    tab = jnp.swapaxes(tab, 3, 4).reshape(depth, ATTN_HEADS // 2, 2, WIN_R, GRID_W, WIN_R * GRID_W)
    return jnp.swapaxes(tab, 2, 3).reshape(depth, ATTN_HEADS // 2, WIN_R, 2 * GRID_W, WIN_R * GRID_W)


def _conv_kernel(am_ref, gm_ref, ap_ref, gp_ref, an_ref, gn_ref, w_ref, cb_ref, lw_ref, lb_ref, o_ref, buf_ref,
                 *, n_prompt_seg):
    seg = pl.program_id(0)
    is_prompt = seg < n_prompt_seg
    half = seg % 2
    has_prev = jnp.logical_and(is_prompt, half == 1)
    has_next = jnp.logical_and(is_prompt, half == 0)

    def glu(a_ref, g_ref):
        return a_ref[...].astype(F32) * jax.nn.sigmoid(g_ref[...].astype(F32))

    buf_ref[pl.ds(HALO, SEG), :] = glu(am_ref, gm_ref)
    buf_ref[pl.ds(0, HALO), :] = jnp.where(has_prev, glu(ap_ref, gp_ref), 0.0)
    buf_ref[pl.ds(HALO + SEG, HALO), :] = jnp.where(has_next, glu(an_ref, gn_ref), 0.0)

    w = w_ref[0]
    off = HALO - CONV_K // 2

    def chunk(ci, carry):
        r0 = pl.multiple_of(ci * CONV_ROWS, CONV_ROWS)
        win_rows = CONV_ROWS + 2 * HALO
        win = buf_ref[pl.ds(r0, win_rows), :]
        acc = jnp.zeros((CONV_ROWS, D_CONV), F32)
        for s in range(8):
            ws = win if s == 0 else pltpu.roll(win, win_rows - s, axis=0)
            for a in range(2 * HALO // 8):
                k = 8 * a + s - off
                if 0 <= k < CONV_K:
                    acc = acc + ws[8 * a:8 * a + CONV_ROWS, :] * w[k:k + 1, :]
        y = acc + cb_ref[0]
        mu = jnp.mean(y, axis=-1, keepdims=True)
        yc = y - mu
        var = jnp.mean(yc * yc, axis=-1, keepdims=True)
        yn = yc * lax.rsqrt(var + EPS) * lw_ref[0] + lb_ref[0]
        o_ref[pl.ds(r0, CONV_ROWS), :] = jax.nn.silu(yn).astype(o_ref.dtype)
        return carry

    lax.fori_loop(0, SEG // CONV_ROWS, chunk, 0)


def _conv(p, conv_w, conv_b, norm_w, norm_b, layer, n_prompt_seg):
    t = p.shape[0]
    depth = conv_w.shape[0]
    n_seg = t // SEG
    ca = 3 * D_ATTN // D_CONV
    cg = ca + 1
    per_seg = SEG // HALO
    last = t // HALO - 1

    def prev_map(col):
        return lambda s: (jnp.maximum(s * per_seg - 1, 0), col)

    def next_map(col):
        return lambda s: (jnp.minimum((s + 1) * per_seg, last), col)

    vec = lambda: pl.BlockSpec((1, 1, D_CONV), lambda s: (layer, 0, 0))
    return pl.pallas_call(
        functools.partial(_conv_kernel, n_prompt_seg=n_prompt_seg),
        grid=(n_seg,),
        in_specs=[
            pl.BlockSpec((SEG, D_CONV), lambda s: (s, ca)),
            pl.BlockSpec((SEG, D_CONV), lambda s: (s, cg)),
            pl.BlockSpec((HALO, D_CONV), prev_map(ca)),
            pl.BlockSpec((HALO, D_CONV), prev_map(cg)),
            pl.BlockSpec((HALO, D_CONV), next_map(ca)),
            pl.BlockSpec((HALO, D_CONV), next_map(cg)),
            pl.BlockSpec((1, CONV_K, D_CONV), lambda s: (layer, 0, 0)),
            vec(), vec(), vec(),
        ],
        out_specs=pl.BlockSpec((SEG, D_CONV), lambda s: (s, 0)),
        out_shape=jax.ShapeDtypeStruct((t, D_CONV), BF16),
        scratch_shapes=[pltpu.VMEM((SEG + 2 * HALO, D_CONV), F32)],
        compiler_params=_cparams(("arbitrary",)),
        name="conv",
    )(p, p, p, p, p, p, conv_w, conv_b.reshape(depth, 1, D_CONV), norm_w.reshape(depth, 1, D_CONV),
      norm_b.reshape(depth, 1, D_CONV))


def _sgu_kernel(u_ref, v_ref, lw_ref, lb_ref, ws_ref, bs_ref, o_ref):
    u = jax.nn.gelu(u_ref[...].astype(F32))
    g = jax.nn.gelu(v_ref[...].astype(F32))
    mu = jnp.mean(g, axis=-1, keepdims=True)
    gc = g - mu
    var = jnp.mean(gc * gc, axis=-1, keepdims=True)
    v = (gc * lax.rsqrt(var + EPS) * lw_ref[0] + lb_ref[0]).astype(BF16)
    lanes = 2 * (D_SGU // SGU_GROUPS)
    first_group = lax.broadcasted_iota(I32, (SGU_CHUNK, lanes), 1) < lanes // 2
    for c in range(TM_SGU // SGU_CHUNK):
        rows = slice(c * SGU_CHUNK, (c + 1) * SGU_CHUNK)
        for pair in range(SGU_GROUPS // 2):
            cols = slice(pair * lanes, (pair + 1) * lanes)
            vp = v[rows, cols]
            m0 = jnp.dot(ws_ref[0, 2 * pair].astype(BF16), vp, preferred_element_type=F32)
            m1 = jnp.dot(ws_ref[0, 2 * pair + 1].astype(BF16), vp, preferred_element_type=F32)
            mixed = jnp.where(first_group, m0, m1) + bs_ref[:, cols]
            o_ref[rows, cols] = (u[rows, cols] * mixed).astype(o_ref.dtype)


def _sgu(p, norm_w, norm_b, sgu_w, bias_exp, layer):
    t = p.shape[0]
    depth = norm_w.shape[0]
    su = (3 * D_ATTN + 2 * D_CONV) // D_SGU
    sv = su + 1
    vec = lambda: pl.BlockSpec((1, 1, D_SGU), lambda i: (layer, 0, 0))
    return pl.pallas_call(
        _sgu_kernel,
        grid=(t // TM_SGU,),
        in_specs=[
            pl.BlockSpec((TM_SGU, D_SGU), lambda i: (i, su)),
            pl.BlockSpec((TM_SGU, D_SGU), lambda i: (i, sv)),
            vec(), vec(),
            pl.BlockSpec((1, SGU_GROUPS, SGU_CHUNK, SGU_CHUNK), lambda i: (layer, 0, 0, 0)),
            pl.BlockSpec((SGU_CHUNK, D_SGU), lambda i: (0, 0)),
        ],
        out_specs=pl.BlockSpec((TM_SGU, D_SGU), lambda i: (i, 0)),
        out_shape=jax.ShapeDtypeStruct((t, D_SGU), BF16),
        compiler_params=_cparams(("arbitrary",)),
        name="sgu",
    )(p, p, norm_w.reshape(depth, 1, D_SGU), norm_b.reshape(depth, 1, D_SGU), sgu_w, bias_exp)


def _rms(y, w):
    return y * lax.rsqrt(jnp.mean(y * y, axis=-1, keepdims=True) + EPS) * w


def _outproj_kernel(ya_ref, yc_ref, ys_ref, x_ref, mw_ref, g1_ref, nw_ref, sh2_ref, sc2_ref, w_ref, rwt_ref, rb_ref,
                    x1_ref, hp_ref, ti_ref, gcol_ref, mix_ref, acc_ref):
    k = pl.program_id(1)
    nk = D_MODEL // TK_OUT

    @pl.when(k == 0)
    def _():
        mw = mw_ref[0]
        na = _rms(ya_ref[...].astype(F32), mw[:, :D_ATTN])
        nc = _rms(yc_ref[...].astype(F32), mw[:, D_ATTN:D_ATTN + D_CONV])
        ns = _rms(ys_ref[...].astype(F32), mw[:, D_ATTN + D_CONV:])
        mix = jnp.concatenate([na, nc, ns], axis=1).astype(BF16)
        for kk in range(nk):
            mix_ref[kk] = mix[:, kk * TK_OUT:(kk + 1) * TK_OUT]
        acc_ref[...] = jnp.zeros_like(acc_ref)

    acc_ref[...] += jnp.dot(mix_ref[k], w_ref[0], preferred_element_type=F32)

    @pl.when(k == nk - 1)
    def _():
        x1 = x_ref[...] + g1_ref[0] * acc_ref[...]
        x1_ref[...] = x1
        h2 = _rms(x1, nw_ref[0]) * (1.0 + sc2_ref[0]) + sh2_ref[0]
        hp_ref[...] = _pack_bf16_pairs(h2)

        hi = h2.astype(BF16)
        lo = (h2 - hi.astype(F32)).astype(BF16)
        rw = rwt_ref[0]
        rwh = rw.astype(BF16)
        rwl = (rw - rwh.astype(F32)).astype(BF16)
        dn = (((1,), (1,)), ((), ()))
        logits = (lax.dot_general(rwh, hi, dn, preferred_element_type=F32)
                  + lax.dot_general(rwh, lo, dn, preferred_element_type=F32)
                  + lax.dot_general(rwl, hi, dn, preferred_element_type=F32)) + rb_ref[0]

        tm = logits.shape[1]
        ie = lax.broadcasted_iota(I32, (N_EXPERTS, tm), 0)
        work = logits
        vals, idxs = [], []
        for _ in range(TOP_K):
            m = jnp.max(work, axis=0, keepdims=True)
            idx = jnp.min(jnp.where(work == m, ie, N_EXPERTS), axis=0, keepdims=True)
            vals.append(m)
            idxs.append(idx)
            work = jnp.where(ie == idx, -jnp.inf, work)
        ex = [jnp.exp(v - vals[0]) for v in vals]
        den = ex[0] + ex[1] + ex[2] + ex[3]
        ti_ref[...] = jnp.concatenate(idxs, axis=0)
        gates = jnp.concatenate([e / den for e in ex] + [jnp.zeros((128 - TOP_K, tm), F32)], axis=0)
        gcol_ref[...] = gates.T


def _outproj(ya, yc, ys, x, mod_rows, mix_w, ffn_w, w_bf16, rwt, rb, layer, n_prompt_seg, nb_pad):
    t = x.shape[0]
    depth = mix_w.shape[0]

    def mod_map(k):
        def f(i, kk):
            b = _batch_of_tile(i, TM_OUT, n_prompt_seg)
            return ((layer * nb_pad + b) * N_MOD + k, 0, 0)
        return f

    lvec = lambda: pl.BlockSpec((1, 1, D_MODEL), lambda i, k: (layer, 0, 0))
    return pl.pallas_call(
        _outproj_kernel,
        grid=(t // TM_OUT, D_MODEL // TK_OUT),
        in_specs=[
            pl.BlockSpec((TM_OUT, D_ATTN), lambda i, k: (i, 0)),
            pl.BlockSpec((TM_OUT, D_CONV), lambda i, k: (i, 0)),
            pl.BlockSpec((TM_OUT, D_SGU), lambda i, k: (i, 0)),
            pl.BlockSpec((TM_OUT, D_MODEL), lambda i, k: (i, 0)),
            lvec(),
            pl.BlockSpec((1, 1, D_MODEL), mod_map(2)),
            lvec(),
            pl.BlockSpec((1, 1, D_MODEL), mod_map(3)),
            pl.BlockSpec((1, 1, D_MODEL), mod_map(4)),
            pl.BlockSpec((1, TK_OUT, D_MODEL), lambda i, k: (layer, k, 0)),
            pl.BlockSpec((1, N_EXPERTS, D_MODEL), lambda i, k: (layer, 0, 0)),
            pl.BlockSpec((1, N_EXPERTS, 1), lambda i, k: (layer, 0, 0)),
        ],
        out_specs=[
            pl.BlockSpec((TM_OUT, D_MODEL), lambda i, k: (i, 0)),
            pl.BlockSpec((TM_OUT, HALF), lambda i, k: (i, 0)),
            pl.BlockSpec((TOP_K, TM_OUT), lambda i, k: (0, i)),
            pl.BlockSpec((TM_OUT, 128), lambda i, k: (i, 0)),
        ],
        out_shape=[
            jax.ShapeDtypeStruct((t, D_MODEL), F32),
            jax.ShapeDtypeStruct((t, HALF), U32),
            jax.ShapeDtypeStruct((TOP_K, t), I32),
            jax.ShapeDtypeStruct((t, 128), F32),
        ],
        scratch_shapes=[
            pltpu.VMEM((D_MODEL // TK_OUT, TM_OUT, TK_OUT), BF16),
            pltpu.VMEM((TM_OUT, D_MODEL), F32),
        ],
        compiler_params=_cparams(("arbitrary", "arbitrary")),
        name="outproj",
    )(ya, yc, ys, x, mix_w.reshape(depth, 1, D_MODEL), mod_rows, ffn_w.reshape(depth, 1, D_MODEL), mod_rows,
      mod_rows, w_bf16, rwt, rb.reshape(depth, N_EXPERTS, 1))


def _rank_kernel(ti_ref, tri_ref, rank_ref, cnt_ref, base_ref):
    i = pl.program_id(0)

    @pl.when(i == 0)
    def _():
        base_ref[...] = jnp.zeros_like(base_ref)

    e = ti_ref[...]
    tt = e.shape[1]
    ie = lax.broadcasted_iota(I32, (N_EXPERTS, tt), 0)
    pre = base_ref[:, 0:1]
    ranks = []
    for s in range(TOP_K):
        hit = ie == e[s:s + 1, :]
        hf = hit.astype(F32)
        earlier = jnp.dot(hf.astype(BF16), tri_ref[...], preferred_element_type=F32)
        ranks.append(jnp.sum(jnp.where(hit, pre + earlier, 0.0), axis=0, keepdims=True))
        pre = pre + jnp.sum(hf, axis=1, keepdims=True)
    rank_ref[...] = jnp.concatenate(ranks, axis=0).astype(I32)
    total = jnp.broadcast_to(pre, base_ref.shape)
    base_ref[...] = total
    cnt_ref[...] = total.astype(I32)


def _rank(ti, tri):
    t = ti.shape[1]
    return pl.pallas_call(
        _rank_kernel,
        grid=(t // TT_RANK,),
        in_specs=[
            pl.BlockSpec((TOP_K, TT_RANK), lambda i: (0, i)),
            pl.BlockSpec((TT_RANK, TT_RANK), lambda i: (0, 0)),
        ],
        out_specs=[
            pl.BlockSpec((TOP_K, TT_RANK), lambda i: (0, i)),
            pl.BlockSpec((N_EXPERTS, 128), lambda i: (0, 0)),
        ],
        out_shape=[
            jax.ShapeDtypeStruct((TOP_K, t), I32),
            jax.ShapeDtypeStruct((N_EXPERTS, 128), I32),
        ],
        scratch_shapes=[pltpu.VMEM((N_EXPERTS, 128), F32)],
        compiler_params=_cparams(("arbitrary",)),
        name="rank",
    )(ti, tri)


def _dispatch_kernel(fill_lo_ref, fill_hi_ref, dest_ref, hp_ref, xs_ref, zero_ref, row_sem, pad_sem):
    i = pl.program_id(0)
    n_fill = fill_lo_ref.shape[0]

    def pad_copy(p):
        return pltpu.make_async_copy(zero_ref.at[pl.ds(0, 1)], xs_ref.at[pl.ds(p, 1)], pad_sem)

    def row_copy(tk, dst):
        return pltpu.make_async_copy(hp_ref.at[pl.ds(tk, 1)], xs_ref.at[pl.ds(dst, 1)], row_sem)

    @pl.when(i == 0)
    def _():
        zero_ref[...] = jnp.zeros_like(zero_ref)
        for f in range(n_fill):
            lo = fill_lo_ref[f]
            hi = fill_hi_ref[f]

            def start(p, c):
                pad_copy(p).start()
                return c

            def wait(p, c):
                pad_copy(p).wait()
                return c

            lax.fori_loop(lo, hi, start, 0)
            lax.fori_loop(lo, hi, wait, 0)

    def issue(tk, c):
        for s in range(TOP_K):
            row_copy(tk, dest_ref[s, tk]).start()
        return c

    def drain(tk, c):
        for s in range(TOP_K):
            row_copy(0, 0).wait()
        return c

    lax.fori_loop(0, TT_DISP, issue, 0, unroll=8)
    lax.fori_loop(0, TT_DISP, drain, 0, unroll=8)


def _dispatch(fill_lo, fill_hi, dest, hp, n_pad):
    t = hp.shape[0]
    grid_spec = pltpu.PrefetchScalarGridSpec(
        num_scalar_prefetch=2,
        grid=(t // TT_DISP,),
        in_specs=[
            pl.BlockSpec((TOP_K, TT_DISP), lambda i, lo, hi: (0, i), memory_space=pltpu.SMEM),
            pl.BlockSpec((TT_DISP, HALF), lambda i, lo, hi: (i, 0)),
        ],
        out_specs=pl.BlockSpec(memory_space=pl.ANY),
        scratch_shapes=[
            pltpu.VMEM((8, HALF), U32),
            pltpu.SemaphoreType.DMA(()),
            pltpu.SemaphoreType.DMA(()),
        ],
    )
    return pl.pallas_call(
        _dispatch_kernel,
        grid_spec=grid_spec,
        out_shape=jax.ShapeDtypeStruct((n_pad, HALF), U32),
        compiler_params=_cparams(("arbitrary",)),
        name="dispatch",
    )(fill_lo, fill_hi, dest, hp)


def _expert_kernel(be_ref, nu_ref, nv_ref, x_ref, wg_ref, wu_ref, bg_ref, bu_ref, wd_ref, bd_ref, y_ref, xb_ref,
                   acc_ref):
    i = pl.program_id(0)
    j = pl.program_id(1)
    nf = D_FF // TF_EXP
    active = i < nu_ref[0]

    @pl.when(jnp.logical_and(active, j == 0))
    def _():
        u = x_ref[...]
        xb_ref[:, :HALF] = _unpack_lo(u).astype(BF16)
        xb_ref[:, HALF:] = _unpack_hi(u).astype(BF16)
        acc_ref[...] = jnp.zeros_like(acc_ref)

    def ffn_rows(m):
        xb = xb_ref[:m, :]
        gate = jnp.dot(xb, wg_ref[0, 0].astype(BF16), preferred_element_type=F32) + bg_ref[0, 0]
        up = jnp.dot(xb, wu_ref[0, 0].astype(BF16), preferred_element_type=F32) + bu_ref[0, 0]
        gate = jnp.minimum(gate, SWIGLU_LIMIT)
        up = jnp.clip(up, -SWIGLU_LIMIT, SWIGLU_LIMIT)
        act = (up + 1.0) * (gate * jax.nn.sigmoid(SWIGLU_ALPHA * gate))
        acc_ref[:m, :] += jnp.dot(act.astype(BF16), wd_ref[0, 0].astype(BF16), preferred_element_type=F32)

    groups = (nv_ref[i] + ROW_STEP - 1) // ROW_STEP
    for g in range(1, TM_EXP // ROW_STEP + 1):
        @pl.when(jnp.logical_and(active, groups == g))
        def _(g=g):
            ffn_rows(g * ROW_STEP)

    @pl.when(jnp.logical_and(active, j == nf - 1))
    def _():
        y_ref[...] = _pack_bf16_pairs(acc_ref[...] + bd_ref[0, 0])

    @pl.when(jnp.logical_and(jnp.logical_not(active), j == nf - 1))
    def _():
        y_ref[...] = jnp.zeros_like(y_ref)


def _experts(blk_e, n_used, nvalid, xs, w_gu, b_gu, w_dn, b_dn, layer):
    n_pad = xs.shape[0]
    depth = w_gu.shape[0]
    nf = D_FF // TF_EXP

    def blk(i, nu):
        return jnp.minimum(i, nu[0] - 1)

    def ftile(i, j, nu):
        return jnp.where(i < nu[0], j, nf - 1)

    grid_spec = pltpu.PrefetchScalarGridSpec(
        num_scalar_prefetch=3,
        grid=(n_pad // TM_EXP, nf),
        in_specs=[
            pl.BlockSpec((TM_EXP, HALF), lambda i, j, be, nu, nv: (blk(i, nu), 0)),
            pl.BlockSpec((1, 1, D_MODEL, TF_EXP),
                         lambda i, j, be, nu, nv: (layer, be[blk(i, nu)], 0, ftile(i, j, nu))),
            pl.BlockSpec((1, 1, D_MODEL, TF_EXP),
                         lambda i, j, be, nu, nv: (layer, be[blk(i, nu)], 0, nf + ftile(i, j, nu))),
            pl.BlockSpec((1, 1, 1, TF_EXP), lambda i, j, be, nu, nv: (layer, be[blk(i, nu)], 0, ftile(i, j, nu))),
            pl.BlockSpec((1, 1, 1, TF_EXP),
                         lambda i, j, be, nu, nv: (layer, be[blk(i, nu)], 0, nf + ftile(i, j, nu))),
            pl.BlockSpec((1, 1, TF_EXP, D_MODEL),
                         lambda i, j, be, nu, nv: (layer, be[blk(i, nu)], ftile(i, j, nu), 0)),
            pl.BlockSpec((1, 1, 1, D_MODEL), lambda i, j, be, nu, nv: (layer, be[blk(i, nu)], 0, 0)),
        ],
        out_specs=pl.BlockSpec((TM_EXP, HALF), lambda i, j, be, nu, nv: (i, 0)),
        scratch_shapes=[
            pltpu.VMEM((TM_EXP, D_MODEL), BF16),
            pltpu.VMEM((TM_EXP, D_MODEL), F32),
        ],
    )
    return pl.pallas_call(
        _expert_kernel,
        grid_spec=grid_spec,
        out_shape=jax.ShapeDtypeStruct((n_pad, HALF), U32),
        compiler_params=_cparams(("arbitrary", "arbitrary")),
        name="experts",
    )(blk_e, n_used, nvalid, xs, w_gu, w_gu, b_gu.reshape(depth, N_EXPERTS, 1, 2 * D_FF),
      b_gu.reshape(depth, N_EXPERTS, 1, 2 * D_FF), w_dn, b_dn.reshape(depth, N_EXPERTS, 1, D_MODEL))


def _combine_kernel(dest_ref, gcol_ref, x_ref, g2_ref, y_ref, o_ref, buf_ref, sem):
    def row_copy(s, tk, src):
        return pltpu.make_async_copy(y_ref.at[pl.ds(src, 1)], buf_ref.at[s, pl.ds(tk, 1)], sem)

    def issue(tk, c):
        for s in range(TOP_K):
            row_copy(s, tk, dest_ref[s, tk]).start()
        return c

    def drain(tk, c):
        for s in range(TOP_K):
            row_copy(s, tk, 0).wait()
        return c

    lax.fori_loop(0, TT_COMB, issue, 0, unroll=8)
    lax.fori_loop(0, TT_COMB, drain, 0, unroll=8)

    g = gcol_ref[...]
    lo = jnp.zeros((TT_COMB, HALF), F32)
    hi = jnp.zeros((TT_COMB, HALF), F32)
    for s in range(TOP_K):
        u = buf_ref[s]
        gs = g[:, s:s + 1]
        lo = lo + _unpack_lo(u) * gs
        hi = hi + _unpack_hi(u) * gs
    g2 = g2_ref[0]
    o_ref[:, :HALF] = x_ref[:, :HALF] + g2[:, :HALF] * lo
    o_ref[:, HALF:] = x_ref[:, HALF:] + g2[:, HALF:] * hi


def _combine(dest, gcol, x1, mod_rows, y, layer, n_prompt_seg, nb_pad):
    t = x1.shape[0]

    def g2_map(i):
        b = _batch_of_tile(i, TT_COMB, n_prompt_seg)
        return ((layer * nb_pad + b) * N_MOD + 5, 0, 0)

    return pl.pallas_call(
        _combine_kernel,
        grid=(t // TT_COMB,),
        in_specs=[
            pl.BlockSpec((TOP_K, TT_COMB), lambda i: (0, i), memory_space=pltpu.SMEM),
            pl.BlockSpec((TT_COMB, 128), lambda i: (i, 0)),
            pl.BlockSpec((TT_COMB, D_MODEL), lambda i: (i, 0)),
            pl.BlockSpec((1, 1, D_MODEL), g2_map),
            pl.BlockSpec(memory_space=pl.ANY),
        ],
        out_specs=pl.BlockSpec((TT_COMB, D_MODEL), lambda i: (i, 0)),
        out_shape=jax.ShapeDtypeStruct((t, D_MODEL), F32),
        scratch_shapes=[
            pltpu.VMEM((TOP_K, TT_COMB, HALF), U32),
            pltpu.SemaphoreType.DMA(()),
        ],
        compiler_params=_cparams(("arbitrary",)),
        name="combine",
    )(dest, gcol, x1, mod_rows, y)


def _routing_plan(ti, rank, counts, n_blocks):
    pcounts = (counts + TM_EXP - 1) // TM_EXP * TM_EXP
    pend = jnp.cumsum(pcounts)
    pstart = pend - pcounts
    experts = jnp.arange(N_EXPERTS, dtype=I32)
    dest = rank + jnp.sum(jnp.where(ti[None] == experts[:, None, None], pstart[:, None, None], 0), axis=0)
    first_row = jnp.arange(n_blocks, dtype=I32) * TM_EXP
    blk_e = jnp.minimum(jnp.sum((pend[None, :] <= first_row[:, None]).astype(I32), axis=1), N_EXPERTS - 1)
    n_used = (pend[-1:] // TM_EXP).astype(I32)
    nvalid = jnp.clip(counts[blk_e] - (first_row - pstart[blk_e]), 0, TM_EXP).astype(I32)
    fill_lo = jnp.concatenate([pstart + counts, pend[-1:]]).astype(I32)
    fill_hi = jnp.concatenate([pend, jnp.full((1,), n_blocks * TM_EXP, I32)]).astype(I32)
    return dest.astype(I32), blk_e, n_used, nvalid, fill_lo, fill_hi


def kernel(x_prompt, x_sample, c_prompt, c_sample, ada_w, ada_b, norm_mix_w, norm_ffn_w, w_in, q_norm_w, k_norm_w, rel_pos_bias, conv_w, conv_b, conv_norm_w, conv_norm_b, sgu_norm_w, sgu_norm_b, sgu_w, sgu_b, mix_norm_w, w_out, router_w, router_b, w_gate_up, b_gate_up, w_down, b_down):
    bp, n_p, d = x_prompt.shape
    bs, n_s, _ = x_sample.shape
    assert d == D_MODEL and n_p == 2 * SEG and n_s == SEG
    depth = ada_w.shape[0]
    n_prompt_seg = 2 * bp
    t = bp * n_p + bs * n_s
    nb = bp + bs
    nb_pad = -(-nb // 8) * 8

    x = jnp.concatenate([x_prompt.reshape(bp * n_p, d), x_sample.reshape(bs * n_s, d)], axis=0)
    c_pad = jnp.concatenate([c_prompt, c_sample, jnp.zeros((nb_pad - nb, d), F32)], axis=0)
    mod_rows = _adaln(c_pad, ada_w, ada_b).reshape(depth * nb_pad * N_MOD, 1, D_MODEL)

    w_in_b = w_in.astype(BF16)
    w_out_b = w_out.astype(BF16)
    rwt = jnp.swapaxes(router_w, 1, 2)
    bias_tab = _bias_table(rel_pos_bias)
    group = np.arange(TN_IN) // ATTN_HD
    ones_bd = jnp.asarray(group[:, None] == group[None, :], BF16)
    tri = jnp.asarray(np.arange(TT_RANK)[:, None] < np.arange(TT_RANK)[None, :], BF16)
    n_slot = t * TOP_K
    n_blocks = -(-(n_slot + N_EXPERTS * (TM_EXP - 1)) // TM_EXP)

    for l in range(depth):
        qn = jnp.tile(q_norm_w[l] * (ATTN_HD ** -0.5), TN_IN // ATTN_HD).reshape(1, TN_IN)
        kn = jnp.tile(k_norm_w[l], TN_IN // ATTN_HD).reshape(1, TN_IN)
        sgu_bias = jnp.repeat(sgu_b[l].T, D_SGU // SGU_GROUPS, axis=1)

        p = _inproj(x, mod_rows, norm_mix_w, w_in_b, qn, kn, ones_bd, l, n_prompt_seg, nb_pad)
        ya = _attention(p, bias_tab, l, n_prompt_seg)
        yc = _conv(p, conv_w, conv_b, conv_norm_w, conv_norm_b, l, n_prompt_seg)
        ys = _sgu(p, sgu_norm_w, sgu_norm_b, sgu_w, sgu_bias, l)
        x1, hp, ti, gcol = _outproj(ya, yc, ys, x, mod_rows, mix_norm_w, norm_ffn_w, w_out_b, rwt, router_b, l,
                                    n_prompt_seg, nb_pad)
        rank, cnt = _rank(ti, tri)
        dest, blk_e, n_used, nvalid, fill_lo, fill_hi = _routing_plan(ti, rank, cnt[:, 0], n_blocks)
        xs = _dispatch(fill_lo, fill_hi, dest, hp, n_blocks * TM_EXP)
        y = _experts(blk_e, n_used, nvalid, xs, w_gate_up, b_gate_up, w_down, b_down, l)
        x = _combine(dest, gcol, x1, mod_rows, y, l, n_prompt_seg, nb_pad)

    y_prompt = x[:bp * n_p].reshape(bp, n_p, d)
    y_sample = x[bp * n_p:].reshape(bs, n_s, d)
    return (y_prompt, y_sample)
```

```python
import functools

import numpy as np
import jax
import jax.numpy as jnp
from jax import lax
from jax.experimental import pallas as pl
from jax.experimental.pallas import tpu as pltpu

F32 = jnp.float32
BF16 = jnp.bfloat16
U32 = jnp.uint32
I32 = jnp.int32

D_MODEL = 2048
GRID_W = 64
ATTN_HEADS = 16
ATTN_HD = 64
D_ATTN = ATTN_HEADS * ATTN_HD
WIN_R = 8
WIN_C = 16
D_CONV = 512
CONV_K = 31
D_SGU = 512
SGU_GROUPS = 8
SGU_CHUNK = 128
D_IN = 3 * D_ATTN + 2 * D_CONV + 2 * D_SGU
N_EXPERTS = 32
TOP_K = 4
D_FF = 2048
SWIGLU_LIMIT = 7.0
SWIGLU_ALPHA = 1.702
N_MOD = 6
EPS = 1e-6

SEG = 2048
SEG_ROWS = SEG // GRID_W
HALF = D_MODEL // 2
NEG_BIAS = -1e30
HI_MASK = 0xFFFF0000

V7X_VMEM_LIMIT_BYTES = 56 * 1024 * 1024

TN_ADA = 1024
TM_IN, TN_IN = 1024, 512
TM_SGU = 1024
TM_OUT, TK_OUT = 512, 1024
TT_RANK = 512
TT_DISP = 1024
TM_EXP, TF_EXP = 1024, 256
ROW_STEP = 128
TT_COMB = 512
ATTN_ROW_GROUP = 16
CONV_ROWS = 128
HALO = 16


def _cparams(sem):
    return pltpu.CompilerParams(dimension_semantics=sem, vmem_limit_bytes=V7X_VMEM_LIMIT_BYTES)


def _batch_of_tile(i, tile, n_prompt_seg):
    seg = (i * tile) // SEG
    return jnp.where(seg < n_prompt_seg, seg // 2, seg - n_prompt_seg // 2)


def _pack_bf16_pairs(y):
    yb = y.astype(BF16)
    lo = pltpu.bitcast(yb[:, :HALF].astype(F32), U32) >> 16
    hi = pltpu.bitcast(yb[:, HALF:].astype(F32), U32) & jnp.uint32(HI_MASK)
    return lo | hi


def _unpack_lo(u):
    return pltpu.bitcast(u << 16, F32)


def _unpack_hi(u):
    return pltpu.bitcast(u & jnp.uint32(HI_MASK), F32)


def _adaln_kernel(c_ref, w_ref, b_ref, o_ref):
    sc = jax.nn.silu(c_ref[...]).astype(BF16)
    o_ref[0] = jnp.dot(sc, w_ref[0].astype(BF16), preferred_element_type=F32) + b_ref[0]


def _adaln(c_pad, ada_w, ada_b):
    depth = ada_w.shape[0]
    nb = c_pad.shape[0]
    n_out = N_MOD * D_MODEL
    return pl.pallas_call(
        _adaln_kernel,
        grid=(depth, n_out // TN_ADA),
        in_specs=[
            pl.BlockSpec((nb, D_MODEL), lambda l, j: (0, 0)),
            pl.BlockSpec((1, D_MODEL, TN_ADA), lambda l, j: (l, 0, j)),
            pl.BlockSpec((1, 1, TN_ADA), lambda l, j: (l, 0, j)),
        ],
        out_specs=pl.BlockSpec((1, nb, TN_ADA), lambda l, j: (l, 0, j)),
        out_shape=jax.ShapeDtypeStruct((depth, nb, n_out), F32),
        compiler_params=_cparams(("arbitrary", "arbitrary")),
        name="adaln",
    )(c_pad, ada_w, ada_b.reshape(depth, 1, n_out))


def _inproj_kernel(x_ref, nw_ref, shift_ref, scale_ref, w_ref, qn_ref, kn_ref, ones_ref, o_ref, h_ref):
    j = pl.program_id(1)

    @pl.when(j == 0)
    def _():
        xf = x_ref[...]
        y = xf * lax.rsqrt(jnp.mean(xf * xf, axis=-1, keepdims=True) + EPS)
        h = (y * nw_ref[0]) * (1.0 + scale_ref[0]) + shift_ref[0]
        h_ref[...] = h.astype(BF16)

    acc = jnp.dot(h_ref[...], w_ref[0], preferred_element_type=F32)
    n_qk = 2 * D_ATTN // TN_IN

    @pl.when(j < n_qk)
    def _():
        ss = jnp.dot((acc * acc).astype(BF16), ones_ref[...], preferred_element_type=F32)
        r = lax.rsqrt(ss * (1.0 / ATTN_HD) + EPS)
        nw = jnp.where(j < n_qk // 2, qn_ref[...], kn_ref[...])
        o_ref[...] = (acc * r * nw).astype(o_ref.dtype)

    @pl.when(j >= n_qk)
    def _():
        o_ref[...] = acc.astype(o_ref.dtype)


def _inproj(x, mod_rows, norm_w, w_bf16, qn, kn, ones_bd, layer, n_prompt_seg, nb_pad):
    t = x.shape[0]
    depth = norm_w.shape[0]

    def mod_map(k):
        def f(i, j):
            b = _batch_of_tile(i, TM_IN, n_prompt_seg)
            return ((layer * nb_pad + b) * N_MOD + k, 0, 0)
        return f

    return pl.pallas_call(
        _inproj_kernel,
        grid=(t // TM_IN, D_IN // TN_IN),
        in_specs=[
            pl.BlockSpec((TM_IN, D_MODEL), lambda i, j: (i, 0)),
            pl.BlockSpec((1, 1, D_MODEL), lambda i, j: (layer, 0, 0)),
            pl.BlockSpec((1, 1, D_MODEL), mod_map(0)),
            pl.BlockSpec((1, 1, D_MODEL), mod_map(1)),
            pl.BlockSpec((1, D_MODEL, TN_IN), lambda i, j: (layer, 0, j)),
            pl.BlockSpec((1, TN_IN), lambda i, j: (0, 0)),
            pl.BlockSpec((1, TN_IN), lambda i, j: (0, 0)),
            pl.BlockSpec((TN_IN, TN_IN), lambda i, j: (0, 0)),
        ],
        out_specs=pl.BlockSpec((TM_IN, TN_IN), lambda i, j: (i, j)),
        out_shape=jax.ShapeDtypeStruct((t, D_IN), BF16),
        scratch_shapes=[pltpu.VMEM((TM_IN, D_MODEL), BF16)],
        compiler_params=_cparams(("arbitrary", "arbitrary")),
        name="inproj",
    )(x, norm_w.reshape(depth, 1, D_MODEL), mod_rows, mod_rows, w_bf16, qn, kn, ones_bd)


def _attn_kernel(q_ref, k_ref, v_ref, b_ref, o_ref, *, n_prompt_seg):
    seg = pl.program_id(0)
    is_prompt = seg < n_prompt_seg
    half = seg % 2
    rows = jnp.where(is_prompt, 2 * SEG_ROWS, SEG_ROWS)
    row0 = jnp.where(is_prompt, half * SEG_ROWS, 0)
    kv0 = jnp.where(is_prompt, 0, half * SEG)
    lane = lax.broadcasted_iota(I32, (GRID_W, 2 * ATTN_HD), 1)
    first_head = lane < ATTN_HD
    band = WIN_R * GRID_W

    def group_body(gi, carry):
        q0s, k0s, scores = [], [], []
        for g in range(ATTN_ROW_GROUP):
            rr = gi * ATTN_ROW_GROUP + g
            r = row0 + rr
            rs = jnp.clip(r - WIN_R // 2, 0, rows - WIN_R)
            di = r - rs
            q0 = pl.multiple_of(rr * GRID_W, GRID_W)
            k0 = pl.multiple_of(kv0 + rs * GRID_W, GRID_W)
            q = q_ref[pl.ds(q0, GRID_W), :]
            zero = jnp.zeros_like(q)
            qm = jnp.concatenate([jnp.where(first_head, q, zero), jnp.where(first_head, zero, q)], axis=0)
            s = lax.dot_general(qm, k_ref[pl.ds(k0, band), :], (((1,), (1,)), ((), ())),
                                preferred_element_type=F32)
            scores.append(s + b_ref[0, 0, di])
            q0s.append(q0)
            k0s.append(k0)
        probs, inv = [], []
        for s in scores:
            m = jnp.max(s, axis=-1, keepdims=True)
            p = jnp.exp(s - m)
            inv.append(1.0 / jnp.sum(p, axis=-1, keepdims=True))
            probs.append(p.astype(BF16))
        for g in range(ATTN_ROW_GROUP):
            o = jnp.dot(probs[g], v_ref[pl.ds(k0s[g], band), :], preferred_element_type=F32) * inv[g]
            o_ref[pl.ds(q0s[g], GRID_W), :] = jnp.where(first_head, o[:GRID_W], o[GRID_W:]).astype(o_ref.dtype)
        return carry

    lax.fori_loop(0, SEG_ROWS // ATTN_ROW_GROUP, group_body, 0)


def _attention(p, bias_tab, layer, n_prompt_seg):
    t = p.shape[0]
    n_seg = t // SEG
    n_pair = ATTN_HEADS // 2
    lanes = 2 * ATTN_HD
    return pl.pallas_call(
        functools.partial(_attn_kernel, n_prompt_seg=n_prompt_seg),
        grid=(n_seg, n_pair),
        in_specs=[
            pl.BlockSpec((SEG, lanes), lambda s, h: (s, h)),
            pl.BlockSpec((2 * SEG, lanes), lambda s, h: (s // 2, n_pair + h)),
            pl.BlockSpec((2 * SEG, lanes), lambda s, h: (s // 2, 2 * n_pair + h)),
            pl.BlockSpec((1, 1, WIN_R, 2 * GRID_W, WIN_R * GRID_W), lambda s, h: (layer, h, 0, 0, 0)),
        ],
        out_specs=pl.BlockSpec((SEG, lanes), lambda s, h: (s, h)),
        out_shape=jax.ShapeDtypeStruct((t, D_ATTN), BF16),
        compiler_params=_cparams(("arbitrary", "arbitrary")),
        name="attention",
    )(p, p, p, bias_tab)


def _bias_table(rel_pos_bias):
    c = np.arange(GRID_W)
    cs = np.clip(c - WIN_C // 2, 0, GRID_W - WIN_C)
    kc = np.arange(GRID_W)
    valid = (kc[None, :] >= cs[:, None]) & (kc[None, :] < cs[:, None] + WIN_C)
    col_rel = np.clip(kc[None, :] - c[:, None] + (WIN_C - 1), 0, 2 * WIN_C - 2)
    depth = rel_pos_bias.shape[0]
    n_col = 2 * WIN_C - 1
    onehot = jnp.asarray(np.arange(n_col)[:, None] == col_rel.reshape(1, -1), F32)
    cols = jnp.einsum('lhrx,xk->lhrk', rel_pos_bias.astype(F32), onehot, precision=lax.Precision.HIGHEST)
    cols = jnp.where(valid.reshape(1, 1, 1, -1), cols, NEG_BIAS)
    cols = cols.reshape(depth, ATTN_HEADS, 2 * WIN_R - 1, GRID_W, GRID_W)
    tab = jnp.stack([cols[:, :, WIN_R - 1 - di:2 * WIN_R - 1 - di] for di in range(WIN_R)], axis=2)
    tab = jnp.swapaxes(tab, 3, 4).reshape(depth, ATTN_HEADS // 2, 2, WIN_R, GRID_W, WIN_R * GRID_W)
    return jnp.swapaxes(tab, 2, 3).reshape(depth, ATTN_HEADS // 2, WIN_R, 2 * GRID_W, WIN_R * GRID_W)


def _conv_kernel(am_ref, gm_ref, ap_ref, gp_ref, an_ref, gn_ref, w_ref, cb_ref, lw_ref, lb_ref, o_ref, buf_ref,
                 *, n_prompt_seg):
    seg = pl.program_id(0)
    is_prompt = seg < n_prompt_seg
    half = seg % 2
    has_prev = jnp.logical_and(is_prompt, half == 1)
    has_next = jnp.logical_and(is_prompt, half == 0)

    def glu(a_ref, g_ref):
        return a_ref[...].astype(F32) * jax.nn.sigmoid(g_ref[...].astype(F32))

    buf_ref[pl.ds(HALO, SEG), :] = glu(am_ref, gm_ref)
    buf_ref[pl.ds(0, HALO), :] = jnp.where(has_prev, glu(ap_ref, gp_ref), 0.0)
    buf_ref[pl.ds(HALO + SEG, HALO), :] = jnp.where(has_next, glu(an_ref, gn_ref), 0.0)

    w = w_ref[0]
    off = HALO - CONV_K // 2

    def chunk(ci, carry):
        r0 = pl.multiple_of(ci * CONV_ROWS, CONV_ROWS)
        win_rows = CONV_ROWS + 2 * HALO
        win = buf_ref[pl.ds(r0, win_rows), :]
        acc = jnp.zeros((CONV_ROWS, D_CONV), F32)
        for s in range(8):
            ws = win if s == 0 else pltpu.roll(win, win_rows - s, axis=0)
            for a in range(2 * HALO // 8):
                k = 8 * a + s - off
                if 0 <= k < CONV_K:
                    acc = acc + ws[8 * a:8 * a + CONV_ROWS, :] * w[k:k + 1, :]
        y = acc + cb_ref[0]
        mu = jnp.mean(y, axis=-1, keepdims=True)
        yc = y - mu
        var = jnp.mean(yc * yc, axis=-1, keepdims=True)
        yn = yc * lax.rsqrt(var + EPS) * lw_ref[0] + lb_ref[0]
        o_ref[pl.ds(r0, CONV_ROWS), :] = jax.nn.silu(yn).astype(o_ref.dtype)
        return carry

    lax.fori_loop(0, SEG // CONV_ROWS, chunk, 0)


def _conv(p, conv_w, conv_b, norm_w, norm_b, layer, n_prompt_seg):
    t = p.shape[0]
    depth = conv_w.shape[0]
    n_seg = t // SEG
    ca = 3 * D_ATTN // D_CONV
    cg = ca + 1
    per_seg = SEG // HALO
    last = t // HALO - 1

    def prev_map(col):
        return lambda s: (jnp.maximum(s * per_seg - 1, 0), col)

    def next_map(col):
        return lambda s: (jnp.minimum((s + 1) * per_seg, last), col)

    vec = lambda: pl.BlockSpec((1, 1, D_CONV), lambda s: (layer, 0, 0))
    return pl.pallas_call(
        functools.partial(_conv_kernel, n_prompt_seg=n_prompt_seg),
        grid=(n_seg,),
        in_specs=[
            pl.BlockSpec((SEG, D_CONV), lambda s: (s, ca)),
            pl.BlockSpec((SEG, D_CONV), lambda s: (s, cg)),
            pl.BlockSpec((HALO, D_CONV), prev_map(ca)),
            pl.BlockSpec((HALO, D_CONV), prev_map(cg)),
            pl.BlockSpec((HALO, D_CONV), next_map(ca)),
            pl.BlockSpec((HALO, D_CONV), next_map(cg)),
            pl.BlockSpec((1, CONV_K, D_CONV), lambda s: (layer, 0, 0)),
            vec(), vec(), vec(),
        ],
        out_specs=pl.BlockSpec((SEG, D_CONV), lambda s: (s, 0)),
        out_shape=jax.ShapeDtypeStruct((t, D_CONV), BF16),
        scratch_shapes=[pltpu.VMEM((SEG + 2 * HALO, D_CONV), F32)],
        compiler_params=_cparams(("arbitrary",)),
        name="conv",
    )(p, p, p, p, p, p, conv_w, conv_b.reshape(depth, 1, D_CONV), norm_w.reshape(depth, 1, D_CONV),
      norm_b.reshape(depth, 1, D_CONV))


def _sgu_kernel(u_ref, v_ref, lw_ref, lb_ref, ws_ref, bs_ref, o_ref):
    u = jax.nn.gelu(u_ref[...].astype(F32))
    g = jax.nn.gelu(v_ref[...].astype(F32))
    mu = jnp.mean(g, axis=-1, keepdims=True)
    gc = g - mu
    var = jnp.mean(gc * gc, axis=-1, keepdims=True)
    v = (gc * lax.rsqrt(var + EPS) * lw_ref[0] + lb_ref[0]).astype(BF16)
    lanes = 2 * (D_SGU // SGU_GROUPS)
    first_group = lax.broadcasted_iota(I32, (SGU_CHUNK, lanes), 1) < lanes // 2
    for c in range(TM_SGU // SGU_CHUNK):
        rows = slice(c * SGU_CHUNK, (c + 1) * SGU_CHUNK)
        for pair in range(SGU_GROUPS // 2):
            cols = slice(pair * lanes, (pair + 1) * lanes)
            vp = v[rows, cols]
            m0 = jnp.dot(ws_ref[0, 2 * pair].astype(BF16), vp, preferred_element_type=F32)
            m1 = jnp.dot(ws_ref[0, 2 * pair + 1].astype(BF16), vp, preferred_element_type=F32)
            mixed = jnp.where(first_group, m0, m1) + bs_ref[:, cols]
            o_ref[rows, cols] = (u[rows, cols] * mixed).astype(o_ref.dtype)


def _sgu(p, norm_w, norm_b, sgu_w, bias_exp, layer):
    t = p.shape[0]
    depth = norm_w.shape[0]
    su = (3 * D_ATTN + 2 * D_CONV) // D_SGU
    sv = su + 1
    vec = lambda: pl.BlockSpec((1, 1, D_SGU), lambda i: (layer, 0, 0))
    return pl.pallas_call(
        _sgu_kernel,
        grid=(t // TM_SGU,),
        in_specs=[
            pl.BlockSpec((TM_SGU, D_SGU), lambda i: (i, su)),
            pl.BlockSpec((TM_SGU, D_SGU), lambda i: (i, sv)),
            vec(), vec(),
            pl.BlockSpec((1, SGU_GROUPS, SGU_CHUNK, SGU_CHUNK), lambda i: (layer, 0, 0, 0)),
            pl.BlockSpec((SGU_CHUNK, D_SGU), lambda i: (0, 0)),
        ],
        out_specs=pl.BlockSpec((TM_SGU, D_SGU), lambda i: (i, 0)),
        out_shape=jax.ShapeDtypeStruct((t, D_SGU), BF16),
        compiler_params=_cparams(("arbitrary",)),
        name="sgu",
    )(p, p, norm_w.reshape(depth, 1, D_SGU), norm_b.reshape(depth, 1, D_SGU), sgu_w, bias_exp)


def _rms(y, w):
    return y * lax.rsqrt(jnp.mean(y * y, axis=-1, keepdims=True) + EPS) * w


def _outproj_kernel(ya_ref, yc_ref, ys_ref, x_ref, mw_ref, g1_ref, nw_ref, sh2_ref, sc2_ref, w_ref, rwt_ref, rb_ref,
                    x1_ref, hp_ref, ti_ref, gcol_ref, mix_ref, acc_ref):
    k = pl.program_id(1)
    nk = D_MODEL // TK_OUT

    @pl.when(k == 0)
    def _():
        mw = mw_ref[0]
        na = _rms(ya_ref[...].astype(F32), mw[:, :D_ATTN])
        nc = _rms(yc_ref[...].astype(F32), mw[:, D_ATTN:D_ATTN + D_CONV])
        ns = _rms(ys_ref[...].astype(F32), mw[:, D_ATTN + D_CONV:])
        mix = jnp.concatenate([na, nc, ns], axis=1).astype(BF16)
        for kk in range(nk):
            mix_ref[kk] = mix[:, kk * TK_OUT:(kk + 1) * TK_OUT]
        acc_ref[...] = jnp.zeros_like(acc_ref)

    acc_ref[...] += jnp.dot(mix_ref[k], w_ref[0], preferred_element_type=F32)

    @pl.when(k == nk - 1)
    def _():
        x1 = x_ref[...] + g1_ref[0] * acc_ref[...]
        x1_ref[...] = x1
        h2 = _rms(x1, nw_ref[0]) * (1.0 + sc2_ref[0]) + sh2_ref[0]
        hp_ref[...] = _pack_bf16_pairs(h2)

        hi = h2.astype(BF16)
        lo = (h2 - hi.astype(F32)).astype(BF16)
        rw = rwt_ref[0]
        rwh = rw.astype(BF16)
        rwl = (rw - rwh.astype(F32)).astype(BF16)
        dn = (((1,), (1,)), ((), ()))
        logits = (lax.dot_general(rwh, hi, dn, preferred_element_type=F32)
                  + lax.dot_general(rwh, lo, dn, preferred_element_type=F32)
                  + lax.dot_general(rwl, hi, dn, preferred_element_type=F32)) + rb_ref[0]

        tm = logits.shape[1]
        ie = lax.broadcasted_iota(I32, (N_EXPERTS, tm), 0)
        work = logits
        vals, idxs = [], []
        for _ in range(TOP_K):
            m = jnp.max(work, axis=0, keepdims=True)
            idx = jnp.min(jnp.where(work == m, ie, N_EXPERTS), axis=0, keepdims=True)
            vals.append(m)
            idxs.append(idx)
            work = jnp.where(ie == idx, -jnp.inf, work)
        ex = [jnp.exp(v - vals[0]) for v in vals]
        den = ex[0] + ex[1] + ex[2] + ex[3]
        ti_ref[...] = jnp.concatenate(idxs, axis=0)
        gates = jnp.concatenate([e / den for e in ex] + [jnp.zeros((128 - TOP_K, tm), F32)], axis=0)
        gcol_ref[...] = gates.T


def _outproj(ya, yc, ys, x, mod_rows, mix_w, ffn_w, w_bf16, rwt, rb, layer, n_prompt_seg, nb_pad):
    t = x.shape[0]
    depth = mix_w.shape[0]

    def mod_map(k):
        def f(i, kk):
            b = _batch_of_tile(i, TM_OUT, n_prompt_seg)
            return ((layer * nb_pad + b) * N_MOD + k, 0, 0)
        return f

    lvec = lambda: pl.BlockSpec((1, 1, D_MODEL), lambda i, k: (layer, 0, 0))
    return pl.pallas_call(
        _outproj_kernel,
        grid=(t // TM_OUT, D_MODEL // TK_OUT),
        in_specs=[
            pl.BlockSpec((TM_OUT, D_ATTN), lambda i, k: (i, 0)),
            pl.BlockSpec((TM_OUT, D_CONV), lambda i, k: (i, 0)),
            pl.BlockSpec((TM_OUT, D_SGU), lambda i, k: (i, 0)),
            pl.BlockSpec((TM_OUT, D_MODEL), lambda i, k: (i, 0)),
            lvec(),
            pl.BlockSpec((1, 1, D_MODEL), mod_map(2)),
            lvec(),
            pl.BlockSpec((1, 1, D_MODEL), mod_map(3)),
            pl.BlockSpec((1, 1, D_MODEL), mod_map(4)),
            pl.BlockSpec((1, TK_OUT, D_MODEL), lambda i, k: (layer, k, 0)),
            pl.BlockSpec((1, N_EXPERTS, D_MODEL), lambda i, k: (layer, 0, 0)),
            pl.BlockSpec((1, N_EXPERTS, 1), lambda i, k: (layer, 0, 0)),
        ],
        out_specs=[
            pl.BlockSpec((TM_OUT, D_MODEL), lambda i, k: (i, 0)),
            pl.BlockSpec((TM_OUT, HALF), lambda i, k: (i, 0)),
            pl.BlockSpec((TOP_K, TM_OUT), lambda i, k: (0, i)),
            pl.BlockSpec((TM_OUT, 128), lambda i, k: (i, 0)),
        ],
        out_shape=[
            jax.ShapeDtypeStruct((t, D_MODEL), F32),
            jax.ShapeDtypeStruct((t, HALF), U32),
            jax.ShapeDtypeStruct((TOP_K, t), I32),
            jax.ShapeDtypeStruct((t, 128), F32),
        ],
        scratch_shapes=[
            pltpu.VMEM((D_MODEL // TK_OUT, TM_OUT, TK_OUT), BF16),
            pltpu.VMEM((TM_OUT, D_MODEL), F32),
        ],
        compiler_params=_cparams(("arbitrary", "arbitrary")),
        name="outproj",
    )(ya, yc, ys, x, mix_w.reshape(depth, 1, D_MODEL), mod_rows, ffn_w.reshape(depth, 1, D_MODEL), mod_rows,
      mod_rows, w_bf16, rwt, rb.reshape(depth, N_EXPERTS, 1))


def _rank_kernel(ti_ref, tri_ref, rank_ref, cnt_ref, base_ref):
    i = pl.program_id(0)

    @pl.when(i == 0)
    def _():
        base_ref[...] = jnp.zeros_like(base_ref)

    e = ti_ref[...]
    tt = e.shape[1]
    ie = lax.broadcasted_iota(I32, (N_EXPERTS, tt), 0)
    pre = base_ref[:, 0:1]
    ranks = []
    for s in range(TOP_K):
        hit = ie == e[s:s + 1, :]
        hf = hit.astype(F32)
        earlier = jnp.dot(hf.astype(BF16), tri_ref[...], preferred_element_type=F32)
        ranks.append(jnp.sum(jnp.where(hit, pre + earlier, 0.0), axis=0, keepdims=True))
        pre = pre + jnp.sum(hf, axis=1, keepdims=True)
    rank_ref[...] = jnp.concatenate(ranks, axis=0).astype(I32)
    total = jnp.broadcast_to(pre, base_ref.shape)
    base_ref[...] = total
    cnt_ref[...] = total.astype(I32)


def _rank(ti, tri):
    t = ti.shape[1]
    return pl.pallas_call(
        _rank_kernel,
        grid=(t // TT_RANK,),
        in_specs=[
            pl.BlockSpec((TOP_K, TT_RANK), lambda i: (0, i)),
            pl.BlockSpec((TT_RANK, TT_RANK), lambda i: (0, 0)),
        ],
        out_specs=[
            pl.BlockSpec((TOP_K, TT_RANK), lambda i: (0, i)),
            pl.BlockSpec((N_EXPERTS, 128), lambda i: (0, 0)),
        ],
        out_shape=[
            jax.ShapeDtypeStruct((TOP_K, t), I32),
            jax.ShapeDtypeStruct((N_EXPERTS, 128), I32),
        ],
        scratch_shapes=[pltpu.VMEM((N_EXPERTS, 128), F32)],
        compiler_params=_cparams(("arbitrary",)),
        name="rank",
    )(ti, tri)


def _dispatch_kernel(fill_lo_ref, fill_hi_ref, dest_ref, hp_ref, xs_ref, zero_ref, row_sem, pad_sem):
    i = pl.program_id(0)
    n_fill = fill_lo_ref.shape[0]

    def pad_copy(p):
        return pltpu.make_async_copy(zero_ref.at[pl.ds(0, 1)], xs_ref.at[pl.ds(p, 1)], pad_sem)

    def row_copy(tk, dst):
        return pltpu.make_async_copy(hp_ref.at[pl.ds(tk, 1)], xs_ref.at[pl.ds(dst, 1)], row_sem)

    @pl.when(i == 0)
    def _():
        zero_ref[...] = jnp.zeros_like(zero_ref)
        for f in range(n_fill):
            lo = fill_lo_ref[f]
            hi = fill_hi_ref[f]

            def start(p, c):
                pad_copy(p).start()
                return c

            def wait(p, c):
                pad_copy(p).wait()
                return c

            lax.fori_loop(lo, hi, start, 0)
            lax.fori_loop(lo, hi, wait, 0)

    def issue(tk, c):
        for s in range(TOP_K):
            row_copy(tk, dest_ref[s, tk]).start()
        return c

    def drain(tk, c):
        for s in range(TOP_K):
            row_copy(0, 0).wait()
        return c

    lax.fori_loop(0, TT_DISP, issue, 0, unroll=8)
    lax.fori_loop(0, TT_DISP, drain, 0, unroll=8)


def _dispatch(fill_lo, fill_hi, dest, hp, n_pad):
    t = hp.shape[0]
    grid_spec = pltpu.PrefetchScalarGridSpec(
        num_scalar_prefetch=2,
        grid=(t // TT_DISP,),
        in_specs=[
            pl.BlockSpec((TOP_K, TT_DISP), lambda i, lo, hi: (0, i), memory_space=pltpu.SMEM),
            pl.BlockSpec((TT_DISP, HALF), lambda i, lo, hi: (i, 0)),
        ],
        out_specs=pl.BlockSpec(memory_space=pl.ANY),
        scratch_shapes=[
            pltpu.VMEM((8, HALF), U32),
            pltpu.SemaphoreType.DMA(()),
            pltpu.SemaphoreType.DMA(()),
        ],
    )
    return pl.pallas_call(
        _dispatch_kernel,
        grid_spec=grid_spec,
        out_shape=jax.ShapeDtypeStruct((n_pad, HALF), U32),
        compiler_params=_cparams(("arbitrary",)),
        name="dispatch",
    )(fill_lo, fill_hi, dest, hp)


def _expert_kernel(be_ref, nu_ref, nv_ref, x_ref, wg_ref, wu_ref, bg_ref, bu_ref, wd_ref, bd_ref, y_ref, xb_ref,
                   acc_ref):
    i = pl.program_id(0)
    j = pl.program_id(1)
    nf = D_FF // TF_EXP
    active = i < nu_ref[0]

    @pl.when(jnp.logical_and(active, j == 0))
    def _():
        u = x_ref[...]
        xb_ref[:, :HALF] = _unpack_lo(u).astype(BF16)
        xb_ref[:, HALF:] = _unpack_hi(u).astype(BF16)
        acc_ref[...] = jnp.zeros_like(acc_ref)

    def ffn_rows(m):
        xb = xb_ref[:m, :]
        gate = jnp.dot(xb, wg_ref[0, 0].astype(BF16), preferred_element_type=F32) + bg_ref[0, 0]
        up = jnp.dot(xb, wu_ref[0, 0].astype(BF16), preferred_element_type=F32) + bu_ref[0, 0]
        gate = jnp.minimum(gate, SWIGLU_LIMIT)
        up = jnp.clip(up, -SWIGLU_LIMIT, SWIGLU_LIMIT)
        act = (up + 1.0) * (gate * jax.nn.sigmoid(SWIGLU_ALPHA * gate))
        acc_ref[:m, :] += jnp.dot(act.astype(BF16), wd_ref[0, 0].astype(BF16), preferred_element_type=F32)

    groups = (nv_ref[i] + ROW_STEP - 1) // ROW_STEP
    for g in range(1, TM_EXP // ROW_STEP + 1):
        @pl.when(jnp.logical_and(active, groups == g))
        def _(g=g):
            ffn_rows(g * ROW_STEP)

    @pl.when(jnp.logical_and(active, j == nf - 1))
    def _():
        y_ref[...] = _pack_bf16_pairs(acc_ref[...] + bd_ref[0, 0])

    @pl.when(jnp.logical_and(jnp.logical_not(active), j == nf - 1))
    def _():
        y_ref[...] = jnp.zeros_like(y_ref)


def _experts(blk_e, n_used, nvalid, xs, w_gu, b_gu, w_dn, b_dn, layer):
    n_pad = xs.shape[0]
    depth = w_gu.shape[0]
    nf = D_FF // TF_EXP

    def blk(i, nu):
        return jnp.minimum(i, nu[0] - 1)

    def ftile(i, j, nu):
        return jnp.where(i < nu[0], j, nf - 1)

    grid_spec = pltpu.PrefetchScalarGridSpec(
        num_scalar_prefetch=3,
        grid=(n_pad // TM_EXP, nf),
        in_specs=[
            pl.BlockSpec((TM_EXP, HALF), lambda i, j, be, nu, nv: (blk(i, nu), 0)),
            pl.BlockSpec((1, 1, D_MODEL, TF_EXP),
                         lambda i, j, be, nu, nv: (layer, be[blk(i, nu)], 0, ftile(i, j, nu))),
            pl.BlockSpec((1, 1, D_MODEL, TF_EXP),
                         lambda i, j, be, nu, nv: (layer, be[blk(i, nu)], 0, nf + ftile(i, j, nu))),
            pl.BlockSpec((1, 1, 1, TF_EXP), lambda i, j, be, nu, nv: (layer, be[blk(i, nu)], 0, ftile(i, j, nu))),
            pl.BlockSpec((1, 1, 1, TF_EXP),
                         lambda i, j, be, nu, nv: (layer, be[blk(i, nu)], 0, nf + ftile(i, j, nu))),
            pl.BlockSpec((1, 1, TF_EXP, D_MODEL),
                         lambda i, j, be, nu, nv: (layer, be[blk(i, nu)], ftile(i, j, nu), 0)),
            pl.BlockSpec((1, 1, 1, D_MODEL), lambda i, j, be, nu, nv: (layer, be[blk(i, nu)], 0, 0)),
        ],
        out_specs=pl.BlockSpec((TM_EXP, HALF), lambda i, j, be, nu, nv: (i, 0)),
        scratch_shapes=[
            pltpu.VMEM((TM_EXP, D_MODEL), BF16),
            pltpu.VMEM((TM_EXP, D_MODEL), F32),
        ],
    )
    return pl.pallas_call(
        _expert_kernel,
        grid_spec=grid_spec,
        out_shape=jax.ShapeDtypeStruct((n_pad, HALF), U32),
        compiler_params=_cparams(("arbitrary", "arbitrary")),
        name="experts",
    )(blk_e, n_used, nvalid, xs, w_gu, w_gu, b_gu.reshape(depth, N_EXPERTS, 1, 2 * D_FF),
      b_gu.reshape(depth, N_EXPERTS, 1, 2 * D_FF), w_dn, b_dn.reshape(depth, N_EXPERTS, 1, D_MODEL))


def _combine_kernel(dest_ref, gcol_ref, x_ref, g2_ref, y_ref, *rest, split_tiles):
    if split_tiles is None:
        o_ref, buf_ref, sem = rest
    else:
        o_first_ref, o_second_ref, buf_ref, sem = rest

    def row_copy(s, tk, src):
        return pltpu.make_async_copy(y_ref.at[pl.ds(src, 1)], buf_ref.at[s, pl.ds(tk, 1)], sem)

    def issue(tk, c):
        for s in range(TOP_K):
            row_copy(s, tk, dest_ref[s, tk]).start()
        return c

    def drain(tk, c):
        for s in range(TOP_K):
            row_copy(s, tk, 0).wait()
        return c

    lax.fori_loop(0, TT_COMB, issue, 0, unroll=8)
    lax.fori_loop(0, TT_COMB, drain, 0, unroll=8)

    g = gcol_ref[...]
    lo = jnp.zeros((TT_COMB, HALF), F32)
    hi = jnp.zeros((TT_COMB, HALF), F32)
    for s in range(TOP_K):
        u = buf_ref[s]
        gs = g[:, s:s + 1]
        lo = lo + _unpack_lo(u) * gs
        hi = hi + _unpack_hi(u) * gs
    g2 = g2_ref[0]

    def write(o_ref):
        o_ref[:, :HALF] = x_ref[:, :HALF] + g2[:, :HALF] * lo
        o_ref[:, HALF:] = x_ref[:, HALF:] + g2[:, HALF:] * hi

    if split_tiles is None:
        write(o_ref)
    else:
        i = pl.program_id(0)

        @pl.when(i < split_tiles)
        def _():
            write(o_first_ref)

        @pl.when(i >= split_tiles)
        def _():
            write(o_second_ref)


def _combine(dest, gcol, x1, mod_rows, y, layer, n_prompt_seg, nb_pad, split_rows=None):
    t = x1.shape[0]

    def g2_map(i):
        b = _batch_of_tile(i, TT_COMB, n_prompt_seg)
        return ((layer * nb_pad + b) * N_MOD + 5, 0, 0)

    if split_rows is None:
        split_tiles = None
        out_specs = pl.BlockSpec((TT_COMB, D_MODEL), lambda i: (i, 0))
        out_shape = jax.ShapeDtypeStruct((t, D_MODEL), F32)
    else:
        split_tiles = split_rows // TT_COMB
        out_specs = [
            pl.BlockSpec((TT_COMB, D_MODEL), lambda i: (jnp.minimum(i, split_tiles - 1), 0)),
            pl.BlockSpec((TT_COMB, D_MODEL), lambda i: (jnp.maximum(i - split_tiles, 0), 0)),
        ]
        out_shape = [
            jax.ShapeDtypeStruct((split_rows, D_MODEL), F32),
            jax.ShapeDtypeStruct((t - split_rows, D_MODEL), F32),
        ]

    return pl.pallas_call(
        functools.partial(_combine_kernel, split_tiles=split_tiles),
        grid=(t // TT_COMB,),
        in_specs=[
            pl.BlockSpec((TOP_K, TT_COMB), lambda i: (0, i), memory_space=pltpu.SMEM),
            pl.BlockSpec((TT_COMB, 128), lambda i: (i, 0)),
            pl.BlockSpec((TT_COMB, D_MODEL), lambda i: (i, 0)),
            pl.BlockSpec((1, 1, D_MODEL), g2_map),
            pl.BlockSpec(memory_space=pl.ANY),
        ],
        out_specs=out_specs,
        out_shape=out_shape,
        scratch_shapes=[
            pltpu.VMEM((TOP_K, TT_COMB, HALF), U32),
            pltpu.SemaphoreType.DMA(()),
        ],
        compiler_params=_cparams(("arbitrary",)),
        name="combine",
    )(dest, gcol, x1, mod_rows, y)


def _routing_plan(ti, rank, counts, n_blocks):
    pcounts = (counts + TM_EXP - 1) // TM_EXP * TM_EXP
    pend = jnp.cumsum(pcounts)
    pstart = pend - pcounts
    experts = jnp.arange(N_EXPERTS, dtype=I32)
    dest = rank + jnp.sum(jnp.where(ti[None] == experts[:, None, None], pstart[:, None, None], 0), axis=0)
    first_row = jnp.arange(n_blocks, dtype=I32) * TM_EXP
    blk_e = jnp.minimum(jnp.sum((pend[None, :] <= first_row[:, None]).astype(I32), axis=1), N_EXPERTS - 1)
    n_used = (pend[-1:] // TM_EXP).astype(I32)
    nvalid = jnp.clip(counts[blk_e] - (first_row - pstart[blk_e]), 0, TM_EXP).astype(I32)
    fill_lo = jnp.concatenate([pstart + counts, pend[-1:]]).astype(I32)
    fill_hi = jnp.concatenate([pend, jnp.full((1,), n_blocks * TM_EXP, I32)]).astype(I32)
    return dest.astype(I32), blk_e, n_used, nvalid, fill_lo, fill_hi


def kernel(x_prompt, x_sample, c_prompt, c_sample, ada_w, ada_b, norm_mix_w, norm_ffn_w, w_in, q_norm_w, k_norm_w, rel_pos_bias, conv_w, conv_b, conv_norm_w, conv_norm_b, sgu_norm_w, sgu_norm_b, sgu_w, sgu_b, mix_norm_w, w_out, router_w, router_b, w_gate_up, b_gate_up, w_down, b_down):
    bp, n_p, d = x_prompt.shape
    bs, n_s, _ = x_sample.shape
    assert d == D_MODEL and n_p == 2 * SEG and n_s == SEG
    depth = ada_w.shape[0]
    n_prompt_seg = 2 * bp
    t = bp * n_p + bs * n_s
    nb = bp + bs
    nb_pad = -(-nb // 8) * 8

    x = jnp.concatenate([x_prompt.reshape(bp * n_p, d), x_sample.reshape(bs * n_s, d)], axis=0)
    c_pad = jnp.concatenate([c_prompt, c_sample, jnp.zeros((nb_pad - nb, d), F32)], axis=0)
    mod_rows = _adaln(c_pad, ada_w, ada_b).reshape(depth * nb_pad * N_MOD, 1, D_MODEL)

    w_in_b = w_in.astype(BF16)
    w_out_b = w_out.astype(BF16)
    rwt = jnp.swapaxes(router_w, 1, 2)
    bias_tab = _bias_table(rel_pos_bias)
    group = np.arange(TN_IN) // ATTN_HD
    ones_bd = jnp.asarray(group[:, None] == group[None, :], BF16)
    tri = jnp.asarray(np.arange(TT_RANK)[:, None] < np.arange(TT_RANK)[None, :], BF16)
    n_slot = t * TOP_K
    n_blocks = -(-(n_slot + N_EXPERTS * (TM_EXP - 1)) // TM_EXP)

    for l in range(depth):
        qn = jnp.tile(q_norm_w[l] * (ATTN_HD ** -0.5), TN_IN // ATTN_HD).reshape(1, TN_IN)
        kn = jnp.tile(k_norm_w[l], TN_IN // ATTN_HD).reshape(1, TN_IN)
        sgu_bias = jnp.repeat(sgu_b[l].T, D_SGU // SGU_GROUPS, axis=1)

        p = _inproj(x, mod_rows, norm_mix_w, w_in_b, qn, kn, ones_bd, l, n_prompt_seg, nb_pad)
        ya = _attention(p, bias_tab, l, n_prompt_seg)
        yc = _conv(p, conv_w, conv_b, conv_norm_w, conv_norm_b, l, n_prompt_seg)
        ys = _sgu(p, sgu_norm_w, sgu_norm_b, sgu_w, sgu_bias, l)
        x1, hp, ti, gcol = _outproj(ya, yc, ys, x, mod_rows, mix_norm_w, norm_ffn_w, w_out_b, rwt, router_b, l,
                                    n_prompt_seg, nb_pad)
        rank, cnt = _rank(ti, tri)
        dest, blk_e, n_used, nvalid, fill_lo, fill_hi = _routing_plan(ti, rank, cnt[:, 0], n_blocks)
        xs = _dispatch(fill_lo, fill_hi, dest, hp, n_blocks * TM_EXP)
        y = _experts(blk_e, n_used, nvalid, xs, w_gate_up, b_gate_up, w_down, b_down, l)
        if l + 1 < depth:
            x = _combine(dest, gcol, x1, mod_rows, y, l, n_prompt_seg, nb_pad)
        else:
            y_prompt, y_sample = _combine(dest, gcol, x1, mod_rows, y, l, n_prompt_seg, nb_pad,
                                          split_rows=bp * n_p)

    return (y_prompt.reshape(bp, n_p, d), y_sample.reshape(bs, n_s, d))
```
